```python
import math
import jax, jax.numpy as jnp
from jax import lax
import numpy as np


D_MODEL = 4096
BATCH = 8
SEQ = 2048
DEPTH = 2

NORM_EPS = 1e-6
ROPE_THETA = 10000.0
Q_BLOCK = 128
N_BRANCHES = 4
BRANCH_WIDTH = D_MODEL // N_BRANCHES

MLA_NOPE_DIM = 128
MLA_ROPE_DIM = 64
MLA_V_DIM = 128
MLA_HEADS = BRANCH_WIDTH // MLA_V_DIM
MLA_Q_RANK = (3 * D_MODEL) // 16
MLA_KV_RANK = 512
MLA_WIDTH = MLA_HEADS * MLA_V_DIM

HYENA_WIDTH = BRANCH_WIDTH
HYENA_ORDER = 2
HYENA_DIRS = 2
HYENA_SHORT = 3
HYENA_BANDS = 16
HYENA_EMB = 1 + 2 * HYENA_BANDS
HYENA_FFN = 64
HYENA_DECAY_TARGET = 1e-2
HYENA_MIN_DECAY = -math.log(HYENA_DECAY_TARGET) / 1.5
HYENA_MAX_DECAY = -math.log(HYENA_DECAY_TARGET) / 0.3

DIFF_QK_DIM = 64
DIFF_V_DIM = 128
DIFF_HEADS = BRANCH_WIDTH // DIFF_V_DIM
DIFF_WIDTH = DIFF_HEADS * DIFF_V_DIM

NA_DIM = 128
NA_HEADS = BRANCH_WIDTH // NA_DIM
NA_WIDTH = NA_HEADS * NA_DIM
GRID_W = 64
NA_ROWS_MAX = 8
NA_COLS = 16

IN_SIZES = (MLA_Q_RANK, MLA_KV_RANK, MLA_ROPE_DIM,
            3 * HYENA_WIDTH,
            DIFF_HEADS * 2 * DIFF_QK_DIM, DIFF_HEADS * 2 * DIFF_QK_DIM, DIFF_WIDTH,
            NA_WIDTH, NA_WIDTH, NA_WIDTH,
            N_BRANCHES * D_MODEL)
IN_COLS = sum(IN_SIZES)

N_GROUPS = 4
EXPERTS_PER_GROUP = 8
N_EXPERTS = N_GROUPS * EXPERTS_PER_GROUP
TOP_K_IN_GROUP = 2
EXPERT_FF = D_MODEL // 8

kernel_name = 'hybrid_mla_hyena_diff_na_hmoe_encoder'


def rmsnorm(x, g):
    xf = x.astype(jnp.float32)
    y = xf * lax.rsqrt(jnp.mean(xf * xf, axis=-1, keepdims=True) + NORM_EPS)
    return (y * g.astype(jnp.float32)).astype(x.dtype)


def rope_tables(length, dim):
    inv = 1.0 / (ROPE_THETA ** (jnp.arange(0, dim, 2, dtype=jnp.float32) / dim))
    ang = jnp.arange(length, dtype=jnp.float32)[:, None] * inv[None, :]
    return jnp.cos(ang), jnp.sin(ang)


def apply_rope(x, cos, sin):
    shape = (cos.shape[0],) + (1,) * (x.ndim - 3) + (cos.shape[1],)
    c = cos.reshape(shape).astype(x.dtype)
    s = sin.reshape(shape).astype(x.dtype)
    x1, x2 = jnp.split(x, 2, axis=-1)
    return jnp.concatenate([x1 * c - x2 * s, x1 * s + x2 * c], axis=-1)


def sweep_query_blocks(fn, *qs):
    b, length = qs[0].shape[:2]
    nb = length // Q_BLOCK
    blocks = tuple(jnp.moveaxis(q.reshape((b, nb, Q_BLOCK) + q.shape[2:]), 1, 0) for q in qs)
    out = jnp.moveaxis(lax.map(lambda blk: fn(*blk), blocks), 0, 1)
    return out.reshape((b, length) + out.shape[3:])


def softmax_f32(s):
    return jax.nn.softmax(s.astype(jnp.float32), axis=-1)


def mla_mixer(c_q, c_kv, k_rope, q_norm_g, kv_norm_g, w_uq, w_ukv, cos, sin):
    b, length, _ = c_q.shape
    q = (rmsnorm(c_q, q_norm_g) @ w_uq).reshape(b, length, MLA_HEADS, MLA_NOPE_DIM + MLA_ROPE_DIM)
    q_nope = q[..., :MLA_NOPE_DIM]
    q_rope = apply_rope(q[..., MLA_NOPE_DIM:], cos, sin)
    kv = (rmsnorm(c_kv, kv_norm_g) @ w_ukv).reshape(b, length, MLA_HEADS, MLA_NOPE_DIM + MLA_V_DIM)
    k_nope, v = kv[..., :MLA_NOPE_DIM], kv[..., MLA_NOPE_DIM:]
    k_rope = apply_rope(k_rope, cos, sin)
    scale = (MLA_NOPE_DIM + MLA_ROPE_DIM) ** -0.5

    def block(qn, qr):
        s = jnp.einsum('bqhd,bkhd->bhqk', qn, k_nope) + jnp.einsum('bqhr,bkr->bhqk', qr, k_rope)
        p = softmax_f32(s * scale).astype(v.dtype)
        return jnp.einsum('bhqk,bkhd->bqhd', p, v)

    return sweep_query_blocks(block, q_nope, q_rope).reshape(b, length, MLA_WIDTH)


def short_conv3(u, w, bias):
    up = jnp.pad(u, ((0, 0), (1, 1), (0, 0)))
    return up[:, :-2] * w[0] + up[:, 1:-1] * w[1] + up[:, 2:] * w[2] + bias


def hyena_filters(length, w1, b1, w2, b2, w3, freq, decay):
    f32 = jnp.float32
    pos = jnp.arange(length, dtype=f32)
    t = (pos / max(length - 1, 1))[:, None]
    bands = jnp.linspace(1e-4, HYENA_BANDS - 1, HYENA_BANDS, dtype=f32)
    ang = (2.0 * math.pi / length) * pos[:, None] * bands[None, :]
    feats = jnp.concatenate([t, jnp.cos(ang), jnp.sin(ang)], axis=-1)
    fr = freq.astype(f32)
    h = jnp.sin(fr[0] * (feats @ w1.astype(f32) + b1.astype(f32)))
    h = jnp.sin(fr[1] * (h @ w2.astype(f32) + b2.astype(f32)))
    h = (h @ w3.astype(f32)).reshape(length, HYENA_ORDER, HYENA_DIRS, HYENA_WIDTH)
    h = h * jnp.exp(-t[:, :, None, None] * jnp.abs(decay.astype(f32)))
    fwd = h[:, :, 0]
    bwd = h[1:, :, 1][::-1]
    kern = jnp.concatenate([fwd, jnp.zeros((1, HYENA_ORDER, HYENA_WIDTH), f32), bwd], axis=0)
    kern = kern / jnp.sum(jnp.abs(kern), axis=0, keepdims=True)
    return jnp.fft.rfft(kern, axis=0)


def fft_long_conv(z, kf):
    length = z.shape[1]
    zf = jnp.fft.rfft(z.astype(jnp.float32), n=2 * length, axis=1)
    y = jnp.fft.irfft(zf * kf[None], n=2 * length, axis=1)[:, :length]
    return y.astype(z.dtype)


def hyena_mixer(u, conv_w, conv_b, w1, b1, w2, b2, w3, freq, decay, bias):
    length = u.shape[1]
    u = short_conv3(u, conv_w, conv_b)
    v, x1, x2 = jnp.split(u, 3, axis=-1)
    kf = hyena_filters(length, w1, b1, w2, b2, w3, freq, decay)
    z = x1 * (fft_long_conv(v, kf[:, 0]) + bias[0] * v)
    return x2 * (fft_long_conv(z, kf[:, 1]) + bias[1] * z)


def diff_mixer(q, k, v, lam, subln_g, lambda_init, cos, sin):
    b, length, _ = q.shape
    q = apply_rope(q.reshape(b, length, DIFF_HEADS, 2, DIFF_QK_DIM), cos, sin)
    k = apply_rope(k.reshape(b, length, DIFF_HEADS, 2, DIFF_QK_DIM), cos, sin)
    v = v.reshape(b, length, DIFF_HEADS, DIFF_V_DIM)
    lf = lam.astype(jnp.float32)
    lam_full = jnp.exp(jnp.sum(lf[0] * lf[1])) - jnp.exp(jnp.sum(lf[2] * lf[3])) + lambda_init
    scale = DIFF_QK_DIM ** -0.5

    def block(qb):
        p = softmax_f32(jnp.einsum('bqhmd,bkhmd->bmhqk', qb, k) * scale)
        a = (p[:, 0] - lam_full * p[:, 1]).astype(v.dtype)
        return jnp.einsum('bhqk,bkhd->bqhd', a, v)

    o = sweep_query_blocks(block, q)
    o = rmsnorm(o, subln_g) * (1.0 - lambda_init)
    return o.reshape(b, length, DIFF_WIDTH)


def na_mixer(q, k, v, rpb):
    b, length, _ = q.shape
    rows = length // GRID_W
    kr = min(NA_ROWS_MAX, rows)
    shp = (b, rows, GRID_W, NA_HEADS, NA_DIM)
    q, k, v = q.reshape(shp), k.reshape(shp), v.reshape(shp)
    cols = jnp.arange(GRID_W)
    col_start = jnp.clip(cols - NA_COLS // 2, 0, GRID_W - NA_COLS)
    col_in = (cols[None, :] >= col_start[:, None]) & (cols[None, :] < col_start[:, None] + NA_COLS)
    col_idx = jnp.clip(cols[None, :] - cols[:, None], -(NA_COLS - 1), NA_COLS - 1) + NA_COLS - 1
    col_bias = rpb[:, :, col_idx].astype(jnp.float32)
    row_start = jnp.clip(jnp.arange(rows) - kr // 2, 0, rows - kr)
    scale = NA_DIM ** -0.5

    def row_block(args):
        r, q_r = args
        start = row_start[r]
        k_r = lax.dynamic_slice_in_dim(k, start, kr, axis=1)
        v_r = lax.dynamic_slice_in_dim(v, start, kr, axis=1)
        row_idx = start + jnp.arange(kr) - r + NA_ROWS_MAX - 1
        bias = jnp.transpose(col_bias[:, row_idx], (0, 2, 1, 3))
        s = jnp.einsum('bqhd,bjkhd->bhqjk', q_r, k_r).astype(jnp.float32) * scale + bias
        s = jnp.where(col_in[:, None, :], s, -jnp.inf)
        p = jax.nn.softmax(s.reshape(b, NA_HEADS, GRID_W, kr * GRID_W), axis=-1)
        p = p.reshape(b, NA_HEADS, GRID_W, kr, GRID_W).astype(v.dtype)
        return jnp.einsum('bhqjk,bjkhd->bqhd', p, v_r)

    out = lax.map(row_block, (jnp.arange(rows), jnp.moveaxis(q, 1, 0)))
    return jnp.moveaxis(out, 0, 1).reshape(b, length, NA_WIDTH)


def hier_moe(x, rg_w, rg_b, re_w, re_b, w_gate, w_up, w_down):
    f32 = jnp.float32
    b, length, _ = x.shape
    g_prob = jax.nn.softmax((x @ rg_w + rg_b).astype(f32), axis=-1)
    g_idx = jnp.argmax(g_prob, axis=-1)
    g_val = jnp.max(g_prob, axis=-1)
    g_onehot = jax.nn.one_hot(g_idx, N_GROUPS, dtype=f32)
    e_logit = (x @ re_w + re_b).astype(f32).reshape(b, length, N_GROUPS, EXPERTS_PER_GROUP)
    e_prob = jax.nn.softmax(jnp.einsum('blge,blg->ble', e_logit, g_onehot), axis=-1)
    top_val, top_idx = lax.top_k(e_prob, TOP_K_IN_GROUP)
    top_val = top_val / jnp.sum(top_val, axis=-1, keepdims=True)
    w_in_group = jnp.sum(jax.nn.one_hot(top_idx, EXPERTS_PER_GROUP, dtype=f32) * top_val[..., None], axis=-2)
    combine = g_onehot[..., None] * (g_val[..., None] * w_in_group)[..., None, :]
    out = jnp.zeros_like(x)
    for g in range(N_GROUPS):
        sl = slice(g * EXPERTS_PER_GROUP, (g + 1) * EXPERTS_PER_GROUP)
        a = jnp.einsum('bld,edf->blef', x, w_gate[sl])
        u = jnp.einsum('bld,edf->blef', x, w_up[sl])
        h = jax.nn.silu(a) * u * combine[:, :, g, :, None].astype(x.dtype)
        out = out + jnp.einsum('blef,efd->bld', h, w_down[sl])
    return out


def setup_inputs(seed: int = 0) -> dict:
    key = jax.random.key(seed)
    ks = iter(jax.random.split(key, 40))
    f32 = jnp.float32

    def nrm(shape, scale):
        return jax.random.normal(next(ks), shape, f32) * scale

    def gain(shape):
        return 1.0 + nrm(shape, 0.02)

    decay0 = jnp.linspace(HYENA_MIN_DECAY, HYENA_MAX_DECAY, HYENA_WIDTH, dtype=f32)
    return {
        'x': nrm((BATCH, SEQ, D_MODEL), 1.0),
        'norm_mix_g': gain((DEPTH, D_MODEL)),
        'w_in': nrm((DEPTH, D_MODEL, IN_COLS), D_MODEL ** -0.5),
        'mla_q_norm_g': gain((DEPTH, MLA_Q_RANK)),
        'mla_kv_norm_g': gain((DEPTH, MLA_KV_RANK)),
        'mla_w_uq': nrm((DEPTH, MLA_Q_RANK, MLA_HEADS * (MLA_NOPE_DIM + MLA_ROPE_DIM)), MLA_Q_RANK ** -0.5),
        'mla_w_ukv': nrm((DEPTH, MLA_KV_RANK, MLA_HEADS * (MLA_NOPE_DIM + MLA_V_DIM)), MLA_KV_RANK ** -0.5),
        'hyena_conv_w': nrm((DEPTH, HYENA_SHORT, 3 * HYENA_WIDTH), HYENA_SHORT ** -0.5),
        'hyena_conv_b': nrm((DEPTH, 3 * HYENA_WIDTH), 0.02),
        'hyena_ffn_w1': nrm((DEPTH, HYENA_EMB, HYENA_FFN), HYENA_EMB ** -0.5),
        'hyena_ffn_b1': nrm((DEPTH, HYENA_FFN), 0.02),
        'hyena_ffn_w2': nrm((DEPTH, HYENA_FFN, HYENA_FFN), HYENA_FFN ** -0.5),
        'hyena_ffn_b2': nrm((DEPTH, HYENA_FFN), 0.02),
        'hyena_ffn_w3': nrm((DEPTH, HYENA_FFN, HYENA_ORDER * HYENA_DIRS * HYENA_WIDTH), HYENA_FFN ** -0.5),
        'hyena_sin_freq': gain((DEPTH, 2, HYENA_FFN)),
        'hyena_decay': decay0 + nrm((DEPTH, HYENA_ORDER, HYENA_DIRS, HYENA_WIDTH), 0.1),
        'hyena_bias': nrm((DEPTH, HYENA_ORDER, HYENA_WIDTH), 0.5),
        'diff_lambda': nrm((DEPTH, 4, DIFF_QK_DIM), 0.1),
        'diff_subln_g': gain((DEPTH, DIFF_V_DIM)),
        'na_rpb': nrm((DEPTH, NA_HEADS, 2 * NA_ROWS_MAX - 1, 2 * NA_COLS - 1), 0.02),
        'w_branch': nrm((DEPTH, N_BRANCHES, BRANCH_WIDTH, D_MODEL), BRANCH_WIDTH ** -0.5),
        'w_out': nrm((DEPTH, D_MODEL, D_MODEL), D_MODEL ** -0.5),
        'norm_ffn_g': gain((DEPTH, D_MODEL)),
        'router_group_w': nrm((DEPTH, D_MODEL, N_GROUPS), D_MODEL ** -0.5),
        'router_group_b': nrm((DEPTH, N_GROUPS), 0.01),
        'router_expert_w': nrm((DEPTH, D_MODEL, N_EXPERTS), D_MODEL ** -0.5),
        'router_expert_b': nrm((DEPTH, N_EXPERTS), 0.01),
        'moe_w_gate': nrm((DEPTH, N_EXPERTS, D_MODEL, EXPERT_FF), D_MODEL ** -0.5),
        'moe_w_up': nrm((DEPTH, N_EXPERTS, D_MODEL, EXPERT_FF), D_MODEL ** -0.5),
        'moe_w_down': nrm((DEPTH, N_EXPERTS, EXPERT_FF, D_MODEL), EXPERT_FF ** -0.5),
        'norm_final_g': gain((D_MODEL,)),
    }


def reference(x, norm_mix_g, w_in, mla_q_norm_g, mla_kv_norm_g, mla_w_uq, mla_w_ukv,
              hyena_conv_w, hyena_conv_b, hyena_ffn_w1, hyena_ffn_b1, hyena_ffn_w2, hyena_ffn_b2,
              hyena_ffn_w3, hyena_sin_freq, hyena_decay, hyena_bias, diff_lambda, diff_subln_g,
              na_rpb, w_branch, w_out, norm_ffn_g, router_group_w, router_group_b,
              router_expert_w, router_expert_b, moe_w_gate, moe_w_up, moe_w_down, norm_final_g):
    b, length, _ = x.shape
    cos, sin = rope_tables(length, MLA_ROPE_DIM)
    split_points = np.cumsum(IN_SIZES)[:-1].tolist()
    for l in range(DEPTH):
        h = rmsnorm(x, norm_mix_g[l])
        (c_q, c_kv, k_rope, u_hy, q_d, k_d, v_d, q_na, k_na, v_na, gate_logit) = jnp.split(
            h @ w_in[l], split_points, axis=-1)
        y_a = mla_mixer(c_q, c_kv, k_rope, mla_q_norm_g[l], mla_kv_norm_g[l],
                        mla_w_uq[l], mla_w_ukv[l], cos, sin)
        y_b = hyena_mixer(u_hy, hyena_conv_w[l], hyena_conv_b[l], hyena_ffn_w1[l], hyena_ffn_b1[l],
                          hyena_ffn_w2[l], hyena_ffn_b2[l], hyena_ffn_w3[l], hyena_sin_freq[l],
                          hyena_decay[l], hyena_bias[l])
        lambda_init = 0.8 - 0.6 * math.exp(-0.3 * l)
        y_c = diff_mixer(q_d, k_d, v_d, diff_lambda[l], diff_subln_g[l], lambda_init, cos, sin)
        y_d = na_mixer(q_na, k_na, v_na, na_rpb[l])
        gates = jax.nn.sigmoid(gate_logit.astype(jnp.float32)).astype(x.dtype)
        gates = gates.reshape(b, length, N_BRANCHES, D_MODEL)
        merged = jnp.zeros_like(x)
        for i, y in enumerate((y_a, y_b, y_c, y_d)):
            merged = merged + gates[:, :, i] * (y @ w_branch[l, i])
        x = x + merged @ w_out[l]
        x = x + hier_moe(rmsnorm(x, norm_ffn_g[l]), router_group_w[l], router_group_b[l],
                         router_expert_w[l], router_expert_b[l], moe_w_gate[l], moe_w_up[l], moe_w_down[l])
    return rmsnorm(x, norm_final_g)
```

```python
import functools
import math

import jax
import jax.numpy as jnp
from jax import lax
from jax.experimental import pallas as pl
from jax.experimental.pallas import tpu as pltpu

F32 = jnp.float32
BF16 = jnp.bfloat16

D_MODEL = 4096
DEPTH = 2
NORM_EPS = 1e-6
ROPE_THETA = 10000.0
BRANCH_WIDTH = D_MODEL // 4
HEAD_DIM = 128
N_HEADS = BRANCH_WIDTH // HEAD_DIM
MLA_ROPE_DIM = 64
MLA_Q_RANK = (3 * D_MODEL) // 16
MLA_KV_RANK = 512
HYENA_WIDTH = BRANCH_WIDTH
HYENA_BANDS = 16
HYENA_DECAY_TARGET = 1e-2
DIFF_QK_DIM = 64
GRID_W = 64
NA_ROWS_MAX = 8
NA_COLS = 16
N_GROUPS = 4
EXPERTS_PER_GROUP = 8
N_EXPERTS = N_GROUPS * EXPERTS_PER_GROUP
EXPERT_FF = D_MODEL // 8
MASK_VALUE = -1e30

OFF_QD, OFF_KD, OFF_VD = 0, 1024, 2048
OFF_QNA, OFF_KNA, OFF_VNA = 3072, 4096, 5120
OFF_HY = 6144
OFF_CQ = 9216
OFF_KR = 9984
OFF_CKV = 10240
N_PROJ = 10752
OFF_GATE = N_PROJ
N_PACKED = N_PROJ + 4 * D_MODEL

MOE_TM = 256


def _params(semantics, vmem_mb):
    return pltpu.CompilerParams(dimension_semantics=semantics, vmem_limit_bytes=vmem_mb << 20)


def _norm(x, g):
    return x * lax.rsqrt(jnp.mean(x * x, axis=-1, keepdims=True) + NORM_EPS) * g


def _rms_kernel(x_ref, g_ref, h_ref):
    h_ref[...] = _norm(x_ref[...], g_ref[...]).astype(h_ref.dtype)


def rms_norm(x, g, out_dtype, tm=256):
    t, d = x.shape
    return pl.pallas_call(
        _rms_kernel,
        out_shape=jax.ShapeDtypeStruct((t, d), out_dtype),
        grid=(t // tm,),
        in_specs=[pl.BlockSpec((tm, d), lambda i: (i, 0)), pl.BlockSpec((1, d), lambda i: (0, 0))],
        out_specs=pl.BlockSpec((tm, d), lambda i: (i, 0)),
        compiler_params=_params(("parallel",), 40),
        name="rms_norm",
    )(x, g.reshape(1, d))


def _combine_rms_kernel(x_ref, a_ref, b_ref, g_ref, *out_refs):
    x = x_ref[...] + a_ref[...] + b_ref[...]
    if len(out_refs) == 2:
        out_refs[0][...] = x
    out_refs[-1][...] = _norm(x, g_ref[...]).astype(out_refs[-1].dtype)


def combine_rms_norm(x, out2, g, out_dtype, emit_x, tm=256):
    t, d = x.shape
    nb = t // tm
    row = pl.BlockSpec((tm, d), lambda i: (i, 0))
    shapes = [jax.ShapeDtypeStruct((t, d), out_dtype)]
    if emit_x:
        shapes = [jax.ShapeDtypeStruct((t, d), F32)] + shapes
    return pl.pallas_call(
        _combine_rms_kernel,
        out_shape=shapes,
        grid=(nb,),
        in_specs=[row, row, pl.BlockSpec((tm, d), lambda i: (i + nb, 0)), pl.BlockSpec((1, d), lambda i: (0, 0))],
        out_specs=[row] * len(shapes),
        compiler_params=_params(("parallel",), 56),
        name="combine_rms_norm",
    )(x, out2, out2, g.reshape(1, d))


def _mm_kernel(a_ref, w_ref, o_ref):
    o_ref[...] = jnp.dot(a_ref[...], w_ref[...], preferred_element_type=F32).astype(o_ref.dtype)


def _mm_res_kernel(a_ref, w_ref, r_ref, o_ref):
    o_ref[...] = r_ref[...] + jnp.dot(a_ref[...], w_ref[...], preferred_element_type=F32)


def matmul(a, w, n_out, out_dtype, tm, tn, res=None, vmem_mb=48, name="matmul"):
    m, k = a.shape
    in_specs = [pl.BlockSpec((tm, k), lambda i, j: (i, 0)), pl.BlockSpec((k, tn), lambda i, j: (0, j))]
    args = [a, w]
    body = _mm_kernel
    if res is not None:
        in_specs.append(pl.BlockSpec((tm, tn), lambda i, j: (i, j)))
        args.append(res)
        body = _mm_res_kernel
    return pl.pallas_call(
        body,
        out_shape=jax.ShapeDtypeStruct((m, n_out), out_dtype),
        grid=(m // tm, n_out // tn),
        in_specs=in_specs,
        out_specs=pl.BlockSpec((tm, tn), lambda i, j: (i, j)),
        compiler_params=_params(("parallel", "parallel"), vmem_mb),
        name=name,
    )(*args)


def _norm_mm_kernel(a_ref, g_ref, w_ref, o_ref):
    an = _norm(a_ref[...].astype(F32), g_ref[...]).astype(BF16)
    o_ref[...] = jnp.dot(an, w_ref[...], preferred_element_type=F32).astype(o_ref.dtype)


def norm_matmul(a, col_block, k, g, w, out_dtype, tm, vmem_mb=40, name="norm_matmul"):
    m = a.shape[0]
    n = w.shape[1]
    return pl.pallas_call(
        _norm_mm_kernel,
        out_shape=jax.ShapeDtypeStruct((m, n), out_dtype),
        grid=(m // tm,),
        in_specs=[pl.BlockSpec((tm, k), lambda i: (i, col_block)),
                  pl.BlockSpec((1, k), lambda i: (0, 0)),
                  pl.BlockSpec((k, n), lambda i: (0, 0))],
        out_specs=pl.BlockSpec((tm, n), lambda i: (i, 0)),
        compiler_params=_params(("parallel",), vmem_mb),
        name=name,
    )(a, g.reshape(1, k), w)


def _rope(x, cos_t, sin_a, sin_b):
    return x * cos_t + pltpu.roll(x, 96, 1) * sin_a + pltpu.roll(x, 32, 1) * sin_b


def rope_tables(length, chunks):
    half = MLA_ROPE_DIM // 2
    inv = 1.0 / (ROPE_THETA ** (jnp.arange(0, MLA_ROPE_DIM, 2, dtype=F32) / MLA_ROPE_DIM))
    ang = jnp.arange(length, dtype=F32)[:, None] * inv[None, :]
    cos, sin = jnp.cos(ang), jnp.sin(ang)
    zero = jnp.zeros((length, half), F32)
    c, sa, sb = [], [], []
    for i in range(2):
        on = i < chunks
        c += [cos, cos] if on else [zero, zero]
        sa += [-sin, zero] if on else [zero, zero]
        sb += [zero, sin] if on else [zero, zero]
    return jnp.concatenate(c, 1), jnp.concatenate(sa, 1), jnp.concatenate(sb, 1)


def _softmax_pv(s, v):
    m = jnp.max(s, axis=-1, keepdims=True)
    p = jnp.exp(s - m)
    l = jnp.sum(p, axis=-1, keepdims=True)
    return jnp.dot(p.astype(BF16), v, preferred_element_type=F32) / l


def _qk(q, k):
    return lax.dot_general(q, k, (((1,), (1,)), ((), ())), preferred_element_type=F32)


def _mla_kernel(q_ref, kn_ref, v_ref, kr_ref, cq_ref, saq_ref, sbq_ref, ck_ref, sak_ref, sbk_ref, o_ref, *, scale):
    q = q_ref[...]
    qr = _rope(q[:, HEAD_DIM:].astype(F32), cq_ref[...], saq_ref[...], sbq_ref[...])
    qf = jnp.concatenate([q[:, :HEAD_DIM], qr.astype(BF16)], axis=1)
    kr = _rope(kr_ref[...].astype(F32), ck_ref[...], sak_ref[...], sbk_ref[...])
    kf = jnp.concatenate([kn_ref[...], kr.astype(BF16)], axis=1)
    o_ref[...] = _softmax_pv(_qk(qf, kf) * scale, v_ref[...]).astype(o_ref.dtype)


def mla_attention(qf, kv, proj, tabs, batch, length, tq):
    t = batch * length
    nq = length // tq
    scale = (HEAD_DIM + MLA_ROPE_DIM) ** -0.5
    tab_q = pl.BlockSpec((tq, HEAD_DIM), lambda b, h, i: (i, 0))
    tab_k = pl.BlockSpec((length, HEAD_DIM), lambda b, h, i: (0, 0))
    return pl.pallas_call(
        functools.partial(_mla_kernel, scale=scale),
        out_shape=jax.ShapeDtypeStruct((t, BRANCH_WIDTH), BF16),
        grid=(batch, N_HEADS, nq),
        in_specs=[pl.BlockSpec((tq, 2 * HEAD_DIM), lambda b, h, i: (b * nq + i, h)),
                  pl.BlockSpec((length, HEAD_DIM), lambda b, h, i: (b, 2 * h)),
                  pl.BlockSpec((length, HEAD_DIM), lambda b, h, i: (b, 2 * h + 1)),
                  pl.BlockSpec((length, HEAD_DIM), lambda b, h, i: (b, OFF_KR // HEAD_DIM)),
                  tab_q, tab_q, tab_q, tab_k, tab_k, tab_k],
        out_specs=pl.BlockSpec((tq, HEAD_DIM), lambda b, h, i: (b * nq + i, h)),
        compiler_params=_params(("parallel", "parallel", "parallel"), 48),
        name="mla_attention",
    )(qf, kv, kv, proj, *tabs, *tabs)


def _diff_kernel(q_ref, k_ref, v_ref, cq_ref, saq_ref, sbq_ref, ck_ref, sak_ref, sbk_ref, lam_ref, g_ref, o_ref,
                 *, scale, out_scale):
    q = _rope(q_ref[...].astype(F32), cq_ref[...], saq_ref[...], sbq_ref[...])
    k = _rope(k_ref[...].astype(F32), ck_ref[...], sak_ref[...], sbk_ref[...]).astype(BF16)
    lane = lax.broadcasted_iota(jnp.int32, q.shape, 1)
    v = v_ref[...]
    o0 = _softmax_pv(_qk(jnp.where(lane < DIFF_QK_DIM, q, 0.0).astype(BF16), k) * scale, v)
    o1 = _softmax_pv(_qk(jnp.where(lane >= DIFF_QK_DIM, q, 0.0).astype(BF16), k) * scale, v)
    o = o0 - lam_ref[...] * o1
    o_ref[...] = (_norm(o, g_ref[...]) * out_scale).astype(o_ref.dtype)


def diff_attention(proj, tabs, lam_full, subln_g, lambda_init, batch, length, tq):
    t = batch * length
    nq = length // tq
    tab_q = pl.BlockSpec((tq, HEAD_DIM), lambda b, h, i: (i, 0))
    tab_k = pl.BlockSpec((length, HEAD_DIM), lambda b, h, i: (0, 0))
    vec = pl.BlockSpec((1, HEAD_DIM), lambda b, h, i: (0, 0))
    return pl.pallas_call(
        functools.partial(_diff_kernel, scale=DIFF_QK_DIM ** -0.5, out_scale=1.0 - lambda_init),
        out_shape=jax.ShapeDtypeStruct((t, BRANCH_WIDTH), BF16),
        grid=(batch, N_HEADS, nq),
        in_specs=[pl.BlockSpec((tq, HEAD_DIM), lambda b, h, i: (b * nq + i, OFF_QD // HEAD_DIM + h)),
                  pl.BlockSpec((length, HEAD_DIM), lambda b, h, i: (b, OFF_KD // HEAD_DIM + h)),
                  pl.BlockSpec((length, HEAD_DIM), lambda b, h, i: (b, OFF_VD // HEAD_DIM + h)),
                  tab_q, tab_q, tab_q, tab_k, tab_k, tab_k, vec, vec],
        out_specs=pl.BlockSpec((tq, HEAD_DIM), lambda b, h, i: (b * nq + i, h)),
        compiler_params=_params(("parallel", "parallel", "parallel"), 48),
        name="diff_attention",
    )(proj, proj, proj, *tabs, *tabs, jnp.full((1, HEAD_DIM), lam_full, F32), subln_g.reshape(1, HEAD_DIM))


def _na_kernel(q_ref, k_ref, v_ref, b_ref, o_ref, *, rows, kr, scale):
    r = pl.program_id(1)
    start = jnp.clip(r - kr // 2, 0, rows - kr)
    base = pl.multiple_of(start * GRID_W, GRID_W)
    outs = []
    for h in range(N_HEADS):
        cols = slice(h * HEAD_DIM, (h + 1) * HEAD_DIM)
        s = _qk(q_ref[:, cols], k_ref[pl.ds(base, kr * GRID_W), cols]) * scale + b_ref[0, h]
        outs.append(_softmax_pv(s, v_ref[pl.ds(base, kr * GRID_W), cols]))
    o_ref[...] = jnp.concatenate(outs, axis=1).astype(o_ref.dtype)


def na_bias_table(rpb, rows):
    kr = min(NA_ROWS_MAX, rows)
    cols = jnp.arange(GRID_W)
    col_start = jnp.clip(cols - NA_COLS // 2, 0, GRID_W - NA_COLS)
    col_in = (cols[None, :] >= col_start[:, None]) & (cols[None, :] < col_start[:, None] + NA_COLS)
    col_idx = jnp.clip(cols[None, :] - cols[:, None], -(NA_COLS - 1), NA_COLS - 1) + NA_COLS - 1
    col_bias = rpb[:, :, col_idx].astype(F32)
    row_start = jnp.clip(jnp.arange(rows) - kr // 2, 0, rows - kr)
    row_idx = row_start[:, None] + jnp.arange(kr)[None, :] - jnp.arange(rows)[:, None] + NA_ROWS_MAX - 1
    bias = jnp.transpose(col_bias[:, row_idx], (1, 0, 3, 2, 4))
    bias = jnp.where(col_in[None, None, :, None, :], bias, MASK_VALUE)
    return bias.reshape(rows, N_HEADS, GRID_W, kr * GRID_W)


def na_attention(proj, bias, batch, length):
    t = batch * length
    rows = length // GRID_W
    kr = min(NA_ROWS_MAX, rows)
    kv_spec = lambda off: pl.BlockSpec((length, BRANCH_WIDTH), lambda b, r: (b, off // BRANCH_WIDTH))
    return pl.pallas_call(
        functools.partial(_na_kernel, rows=rows, kr=kr, scale=HEAD_DIM ** -0.5),
        out_shape=jax.ShapeDtypeStruct((t, BRANCH_WIDTH), BF16),
        grid=(batch, rows),
        in_specs=[pl.BlockSpec((GRID_W, BRANCH_WIDTH), lambda b, r: (b * rows + r, OFF_QNA // BRANCH_WIDTH)),
                  kv_spec(OFF_KNA), kv_spec(OFF_VNA),
                  pl.BlockSpec((1, N_HEADS, GRID_W, kr * GRID_W), lambda b, r: (r, 0, 0, 0))],
        out_specs=pl.BlockSpec((GRID_W, BRANCH_WIDTH), lambda b, r: (b * rows + r, 0)),
        compiler_params=_params(("parallel", "parallel"), 40),
        name="na_attention",
    )(proj, proj, proj, bias)


HY_TC = 512


def _short_conv_kernel(u_ref, w_ref, b_ref, o_ref):
    u = u_ref[...].astype(F32)
    n = u.shape[0]
    row = lax.broadcasted_iota(jnp.int32, u.shape, 0)
    prev = jnp.where(row == 0, 0.0, pltpu.roll(u, 1, 0))
    nxt = jnp.where(row == n - 1, 0.0, pltpu.roll(u, n - 1, 0))
    w = w_ref[...]
    o_ref[...] = (prev * w[0:1] + u * w[1:2] + nxt * w[2:3] + b_ref[...]).astype(o_ref.dtype)


def short_conv(proj, w, bias, batch, length):
    t = batch * length
    width = 3 * HYENA_WIDTH
    return pl.pallas_call(
        _short_conv_kernel,
        out_shape=jax.ShapeDtypeStruct((t, width), BF16),
        grid=(batch, width // HY_TC),
        in_specs=[pl.BlockSpec((length, HY_TC), lambda b, c: (b, OFF_HY // HY_TC + c)),
                  pl.BlockSpec((3, HY_TC), lambda b, c: (0, c)),
                  pl.BlockSpec((1, HY_TC), lambda b, c: (0, c))],
        out_specs=pl.BlockSpec((length, HY_TC), lambda b, c: (b, c)),
        compiler_params=_params(("parallel", "parallel"), 48),
        name="short_conv",
    )(proj, w, bias.reshape(1, width))


def dft_matrices(length):
    n = 2 * length
    kb = min(512, length)
    nkb = length // kb
    k = jnp.arange(length, dtype=jnp.int32)[:, None]
    s = jnp.arange(n, dtype=jnp.int32)[None, :]
    ang = ((k * s) % n).astype(F32) * (2.0 * math.pi / n)
    c, sn = jnp.cos(ang), jnp.sin(ang)
    alt = jnp.where(s % 2 == 0, 1.0, -1.0).astype(F32)
    f_re = c
    f_im = jnp.where(k == 0, alt, -sn)
    f_full = jnp.concatenate([f_re, f_im], axis=0).astype(BF16)
    f_fwd = jnp.stack([f_re[:, :length].reshape(nkb, kb, length), f_im[:, :length].reshape(nkb, kb, length)],
                      axis=1).reshape(n, length).astype(BF16)
    kk = jnp.arange(length, dtype=jnp.int32)[None, :]
    g_c = jnp.where(kk == 0, 1.0 / n, c[:, :length].T * (2.0 / n))
    g_s = jnp.where(kk == 0, alt[0, :length][:, None] / n, -sn[:, :length].T * (2.0 / n))
    g_inv = jnp.stack([g_c.reshape(length, nkb, kb), g_s.reshape(length, nkb, kb)],
                      axis=2).reshape(length, n).astype(BF16)
    return f_full, f_fwd, g_inv


def hyena_filter_spectrum(length, w1, b1, w2, b2, w3, freq, decay, f_full):
    hp = lax.Precision.HIGHEST
    pos = jnp.arange(length, dtype=F32)
    tt = (pos / max(length - 1, 1))[:, None]
    bands = jnp.linspace(1e-4, HYENA_BANDS - 1, HYENA_BANDS, dtype=F32)
    ang = (2.0 * math.pi / length) * pos[:, None] * bands[None, :]
    feats = jnp.concatenate([tt, jnp.cos(ang), jnp.sin(ang)], axis=-1)
    h = jnp.sin(freq[0] * (jnp.dot(feats, w1, precision=hp) + b1))
    h = jnp.sin(freq[1] * (jnp.dot(h, w2, precision=hp) + b2))
    h = jnp.dot(h, w3, precision=hp).reshape(length, 2, 2, HYENA_WIDTH)
    h = h * jnp.exp(-tt[:, :, None, None] * jnp.abs(decay))
    fwd = h[:, :, 0]
    bwd = h[1:, :, 1][::-1]
    kern = jnp.concatenate([fwd, jnp.zeros((1, 2, HYENA_WIDTH), F32), bwd], axis=0)
    kern = kern / jnp.sum(jnp.abs(kern), axis=0, keepdims=True)
    n = 2 * length
    tile = min(1024, n)
    spec = matmul(f_full, kern.reshape(n, 2 * HYENA_WIDTH).astype(BF16), 2 * HYENA_WIDTH, F32,
                  tm=tile, tn=min(512, 2 * HYENA_WIDTH), name="filter_dft")
    k_re, k_im = spec[:length], spec[length:]
    first = jnp.arange(length)[:, None] == 0
    return k_re, jnp.where(first, 0.0, k_im), jnp.where(first, k_im[0:1], k_re)


def _dft_fwd_kernel(f_ref, z_ref, a_ref, b_ref, a2_ref, o_ref):
    res = jnp.dot(f_ref[...], z_ref[...], preferred_element_type=F32)
    kb = res.shape[0] // 2
    z_re, z_im = res[:kb], res[kb:]
    b = b_ref[...]
    o_ref[:kb, :] = (z_re * a_ref[...] - z_im * b).astype(o_ref.dtype)
    o_ref[kb:, :] = (z_re * b + z_im * a2_ref[...]).astype(o_ref.dtype)


def dft_forward(f_fwd, z, z_col, coefs, order, batch, length):
    n = 2 * length
    kb = min(512, length)
    nkb = length // kb
    nc = HYENA_WIDTH // HY_TC
    coef = pl.BlockSpec((kb, HY_TC), lambda b, c, k: (k, order * nc + c))
    return pl.pallas_call(
        _dft_fwd_kernel,
        out_shape=jax.ShapeDtypeStruct((batch * n, HYENA_WIDTH), BF16),
        grid=(batch, nc, nkb),
        in_specs=[pl.BlockSpec((2 * kb, length), lambda b, c, k: (k, 0)),
                  pl.BlockSpec((length, HY_TC), lambda b, c, k: (b, z_col + c)),
                  coef, coef, coef],
        out_specs=pl.BlockSpec((2 * kb, HY_TC), lambda b, c, k: (b * nkb + k, c)),
        compiler_params=_params(("parallel", "parallel", "parallel"), 40),
        name="dft_forward",
    )(f_fwd, z, *coefs)


def _dft_inv_kernel(g_ref, y_ref, x_ref, z_ref, bias_ref, o_ref):
    y = jnp.dot(g_ref[...], y_ref[...], preferred_element_type=F32)
    o_ref[...] = (x_ref[...].astype(F32) * (y + bias_ref[...] * z_ref[...].astype(F32))).astype(o_ref.dtype)


def dft_inverse(g_inv, spec, gate, gate_col, z, z_col, bias, batch, length):
    n = 2 * length
    tt = min(512, length)
    nt = length // tt
    nc = HYENA_WIDTH // HY_TC
    return pl.pallas_call(
        _dft_inv_kernel,
        out_shape=jax.ShapeDtypeStruct((batch * length, HYENA_WIDTH), BF16),
        grid=(batch, nc, nt),
        in_specs=[pl.BlockSpec((tt, n), lambda b, c, i: (i, 0)),
                  pl.BlockSpec((n, HY_TC), lambda b, c, i: (b, c)),
                  pl.BlockSpec((tt, HY_TC), lambda b, c, i: (b * nt + i, gate_col + c)),
                  pl.BlockSpec((tt, HY_TC), lambda b, c, i: (b * nt + i, z_col + c)),
                  pl.BlockSpec((1, HY_TC), lambda b, c, i: (0, c))],
        out_specs=pl.BlockSpec((tt, HY_TC), lambda b, c, i: (b * nt + i, c)),
        compiler_params=_params(("parallel", "parallel", "parallel"), 40),
        name="dft_inverse",
    )(g_inv, spec, gate, z, bias.reshape(1, HYENA_WIDTH))


def hyena_mixer(proj, conv_w, conv_b, coefs, bias, f_fwd, g_inv, batch, length):
    nc = HYENA_WIDTH // HY_TC
    cv = short_conv(proj, conv_w, conv_b, batch, length)
    spec = dft_forward(f_fwd, cv, 0, coefs, 0, batch, length)
    z1 = dft_inverse(g_inv, spec, cv, nc, cv, 0, bias[0], batch, length)
    spec = dft_forward(f_fwd, z1, 0, coefs, 1, batch, length)
    return dft_inverse(g_inv, spec, cv, 2 * nc, z1, 0, bias[1], batch, length)


def _merge_kernel(h_ref, g0, g1, g2, g3, y0, y1, y2, y3, wb_ref, o_ref):
    h = h_ref[...]
    acc = None
    for i, (g_ref, y_ref) in enumerate(((g0, y0), (g1, y1), (g2, y2), (g3, y3))):
        gate = jax.nn.sigmoid(jnp.dot(h, g_ref[...], preferred_element_type=F32))
        term = gate * jnp.dot(y_ref[...], wb_ref[i], preferred_element_type=F32)
        acc = term if acc is None else acc + term
    o_ref[...] = acc.astype(o_ref.dtype)


def gated_merge(h, w_packed, ys, w_branch, tm=512, tn=256):
    t, d = h.shape
    gate_spec = lambda i: pl.BlockSpec((d, tn), lambda n, m: (0, (OFF_GATE + i * d) // tn + n))
    y_spec = pl.BlockSpec((tm, BRANCH_WIDTH), lambda n, m: (m, 0))
    return pl.pallas_call(
        _merge_kernel,
        out_shape=jax.ShapeDtypeStruct((t, d), BF16),
        grid=(d // tn, t // tm),
        in_specs=[pl.BlockSpec((tm, d), lambda n, m: (m, 0))] + [gate_spec(i) for i in range(4)] + [y_spec] * 4
                 + [pl.BlockSpec((4, BRANCH_WIDTH, tn), lambda n, m: (0, 0, n))],
        out_specs=pl.BlockSpec((tm, tn), lambda n, m: (m, n)),
        compiler_params=_params(("parallel", "parallel"), 52),
        name="gated_merge",
    )(h, w_packed, w_packed, w_packed, w_packed, *ys, w_branch)


def moe_routing(logits, rg_b, re_b, tm, n_tiles):
    t = logits.shape[0]
    g_prob = jax.nn.softmax(logits[:, :N_GROUPS] + rg_b, axis=-1)
    g_idx = jnp.argmax(g_prob, axis=-1)
    g_val = jnp.max(g_prob, axis=-1)
    e_logit = (logits[:, N_GROUPS:N_GROUPS + N_EXPERTS] + re_b).reshape(t, N_GROUPS, EXPERTS_PER_GROUP)
    e_sel = jnp.take_along_axis(e_logit, g_idx[:, None, None], axis=1)[:, 0]
    top_val, top_idx = lax.top_k(jax.nn.softmax(e_sel, axis=-1), 2)
    top_val = top_val / jnp.sum(top_val, axis=-1, keepdims=True)
    weight = (g_val[:, None] * top_val).T.reshape(-1)
    expert = (g_idx[:, None] * EXPERTS_PER_GROUP + top_idx).T.reshape(-1).astype(jnp.int32)

    order = jnp.argsort(expert, stable=True).astype(jnp.int32)
    counts = jnp.sum(expert[:, None] == jnp.arange(N_EXPERTS, dtype=jnp.int32)[None, :], axis=0, dtype=jnp.int32)
    padded = ((counts + tm - 1) // tm) * tm
    pad_end = jnp.cumsum(padded)
    pad_start = pad_end - padded
    start = jnp.cumsum(counts) - counts
    tile_expert = jnp.minimum(jnp.searchsorted(pad_end, jnp.arange(n_tiles, dtype=jnp.int32) * tm, side="right"),
                              N_EXPERTS - 1).astype(jnp.int32)
    n_used = (pad_end[-1] // tm).astype(jnp.int32).reshape(1)
    row_expert = jnp.repeat(tile_expert, tm)
    rows = jnp.arange(n_tiles * tm, dtype=jnp.int32)
    off = rows - pad_start[row_expert]
    valid = (off < counts[row_expert]) & (rows < pad_end[-1])
    src = order[jnp.clip(start[row_expert] + off, 0, 2 * t - 1)]
    row_tok = jnp.where(valid, src % t, 0).astype(jnp.int32)
    row_dst = jnp.where(valid, src, -1).astype(jnp.int32)
    row_w = jnp.where(valid, weight[src], 0.0).astype(F32)
    return tile_expert, n_used, row_tok, row_dst, row_w


def _moe_kernel(te_ref, nu_ref, tok_ref, dst_ref, w_ref, x_hbm, g_ref, wg_ref, wu_ref, wd_ref, o_hbm,
                xbuf, obuf, sems):
    tm = xbuf.shape[0]

    def gather(r):
        return pltpu.make_async_copy(x_hbm.at[pl.ds(tok_ref[0, 0, r], 1)], xbuf.at[pl.ds(r, 1)], sems.at[0])

    def scatter(r):
        return pltpu.make_async_copy(obuf.at[pl.ds(r, 1)], o_hbm.at[pl.ds(jnp.maximum(dst_ref[0, 0, r], 0), 1)],
                                     sems.at[1])

    @pl.when(pl.program_id(0) < nu_ref[0])
    def _():
        @pl.loop(0, tm)
        def _(r):
            gather(r).start()

        @pl.loop(0, tm)
        def _(r):
            gather(r).wait()

        xn = _norm(xbuf[...], g_ref[...]).astype(BF16)
        a = jnp.dot(xn, wg_ref[0], preferred_element_type=F32)
        u = jnp.dot(xn, wu_ref[0], preferred_element_type=F32)
        hid = (a * jax.nn.sigmoid(a) * u * w_ref[...]).astype(BF16)
        obuf[...] = jnp.dot(hid, wd_ref[0], preferred_element_type=F32)

        @pl.loop(0, tm)
        def _(r):
            @pl.when(dst_ref[0, 0, r] >= 0)
            def _():
                scatter(r).start()

        @pl.loop(0, tm)
        def _(r):
            @pl.when(dst_ref[0, 0, r] >= 0)
            def _():
                scatter(r).wait()


def moe_experts(x, g, schedule, w_gate, w_up, w_down, tm):
    t, d = x.shape
    tile_expert, n_used, row_tok, row_dst, row_w = schedule
    n_tiles = tile_expert.shape[0]
    idx_spec = pl.BlockSpec((1, 1, tm), lambda i, te, nu: (i, 0, 0), memory_space=pltpu.SMEM)
    grid_spec = pltpu.PrefetchScalarGridSpec(
        num_scalar_prefetch=2,
        grid=(n_tiles,),
        in_specs=[idx_spec, idx_spec,
                  pl.BlockSpec((tm, 1), lambda i, te, nu: (i, 0)),
                  pl.BlockSpec(memory_space=pl.ANY),
                  pl.BlockSpec((1, d), lambda i, te, nu: (0, 0)),
                  pl.BlockSpec((1, d, EXPERT_FF), lambda i, te, nu: (te[i], 0, 0)),
                  pl.BlockSpec((1, d, EXPERT_FF), lambda i, te, nu: (te[i], 0, 0)),
                  pl.BlockSpec((1, EXPERT_FF, d), lambda i, te, nu: (te[i], 0, 0))],
        out_specs=pl.BlockSpec(memory_space=pl.ANY),
        scratch_shapes=[pltpu.VMEM((tm, d), F32), pltpu.VMEM((tm, d), F32), pltpu.SemaphoreType.DMA((2,))],
    )
    return pl.pallas_call(
        _moe_kernel,
        out_shape=jax.ShapeDtypeStruct((2 * t, d), F32),
        grid_spec=grid_spec,
        compiler_params=_params(("arbitrary",), 48),
        name="moe_experts",
    )(tile_expert, n_used, row_tok.reshape(n_tiles, 1, tm), row_dst.reshape(n_tiles, 1, tm),
      row_w.reshape(n_tiles * tm, 1), x, g.reshape(1, d), w_gate, w_up, w_down)


def pack_w_in(w):
    d = w.shape[0]
    s = [0, 768, 1280, 1344, 4416, 5440, 6464, 7488, 8512, 9536, 10560]
    c_q, c_kv, k_rope, u_hy = w[:, s[0]:s[1]], w[:, s[1]:s[2]], w[:, s[2]:s[3]], w[:, s[3]:s[4]]
    z = lambda n: jnp.zeros((d, n), w.dtype)
    return jnp.concatenate([w[:, s[4]:s[10]], u_hy, c_q, k_rope, z(64), z(128), c_kv, w[:, s[10]:]],
                           axis=1).astype(BF16)


def pack_w_uq(w):
    r = w.shape[0]
    w = w.reshape(r, N_HEADS, HEAD_DIM + MLA_ROPE_DIM)
    w = jnp.concatenate([w, jnp.zeros((r, N_HEADS, HEAD_DIM - MLA_ROPE_DIM), w.dtype)], axis=-1)
    return w.reshape(r, N_HEADS * 2 * HEAD_DIM).astype(BF16)


def pack_router(rg_w, re_w):
    d = rg_w.shape[0]
    pad = jnp.zeros((d, HEAD_DIM - N_GROUPS - N_EXPERTS), rg_w.dtype)
    return jnp.concatenate([rg_w, re_w, pad], axis=1).astype(BF16)


def kernel(x, norm_mix_g, w_in, mla_q_norm_g, mla_kv_norm_g, mla_w_uq, mla_w_ukv, hyena_conv_w, hyena_conv_b, hyena_ffn_w1, hyena_ffn_b1, hyena_ffn_w2, hyena_ffn_b2, hyena_ffn_w3, hyena_sin_freq, hyena_decay, hyena_bias, diff_lambda, diff_subln_g, na_rpb, w_branch, w_out, norm_ffn_g, router_group_w, router_group_b, router_expert_w, router_expert_b, moe_w_gate, moe_w_up, moe_w_down, norm_final_g):
    batch, length, d = x.shape
    t = batch * length
    assert d == D_MODEL and length % GRID_W == 0 and t % 1024 == 0
    tq = min(512, length)
    n_tiles = 2 * t // MOE_TM + N_EXPERTS

    rope_mla = rope_tables(length, 1)
    rope_diff = rope_tables(length, 2)
    f_full, f_fwd, g_inv = dft_matrices(length)

    x2 = x.reshape(t, d)
    h = rms_norm(x2, norm_mix_g[0], BF16)
    out = None
    for l in range(DEPTH):
        w_packed = pack_w_in(w_in[l])
        proj = matmul(h, w_packed, N_PROJ, BF16, tm=1024, tn=512, name="in_proj")

        qf = norm_matmul(proj, OFF_CQ // MLA_Q_RANK, MLA_Q_RANK, mla_q_norm_g[l], pack_w_uq(mla_w_uq[l]), BF16,
                         tm=512, name="mla_q_up")
        kv = norm_matmul(proj, OFF_CKV // MLA_KV_RANK, MLA_KV_RANK, mla_kv_norm_g[l], mla_w_ukv[l].astype(BF16), BF16,
                         tm=512, name="mla_kv_up")
        y_a = mla_attention(qf, kv, proj, rope_mla, batch, length, tq)

        coefs = hyena_filter_spectrum(length, hyena_ffn_w1[l], hyena_ffn_b1[l], hyena_ffn_w2[l], hyena_ffn_b2[l],
                                      hyena_ffn_w3[l], hyena_sin_freq[l], hyena_decay[l], f_full)
        y_b = hyena_mixer(proj, hyena_conv_w[l], hyena_conv_b[l], coefs, hyena_bias[l], f_fwd, g_inv, batch, length)

        lambda_init = 0.8 - 0.6 * math.exp(-0.3 * l)
        lam = diff_lambda[l].astype(F32)
        lam_full = jnp.exp(jnp.sum(lam[0] * lam[1])) - jnp.exp(jnp.sum(lam[2] * lam[3])) + lambda_init
        y_c = diff_attention(proj, rope_diff, lam_full, diff_subln_g[l], lambda_init, batch, length, tq)

        y_d = na_attention(proj, na_bias_table(na_rpb[l], length // GRID_W), batch, length)

        merged = gated_merge(h, w_packed, (y_a, y_b, y_c, y_d), w_branch[l].astype(BF16))
        x_mid = matmul(merged, w_out[l].astype(BF16), d, F32, tm=1024, tn=512, res=x2, name="out_proj")

        logits = norm_matmul(x_mid, 0, d, norm_ffn_g[l], pack_router(router_group_w[l], router_expert_w[l]), F32,
                             tm=256, name="router")
        schedule = moe_routing(logits, router_group_b[l], router_expert_b[l], MOE_TM, n_tiles)
        out2 = moe_experts(x_mid, norm_ffn_g[l], schedule, moe_w_gate[l].astype(BF16), moe_w_up[l].astype(BF16),
                           moe_w_down[l].astype(BF16), MOE_TM)
        if l + 1 < DEPTH:
            x2, h = combine_rms_norm(x_mid, out2, norm_mix_g[l + 1], BF16, emit_x=True)
        else:
            out = combine_rms_norm(x_mid, out2, norm_final_g, F32, emit_x=False)[0]
    return out.reshape(batch, length, d)
```

```python
import functools
import math

import jax
import jax.numpy as jnp
from jax import lax
from jax.experimental import pallas as pl
from jax.experimental.pallas import tpu as pltpu

F32 = jnp.float32
BF16 = jnp.bfloat16

D_MODEL = 4096
DEPTH = 2
NORM_EPS = 1e-6
ROPE_THETA = 10000.0
BRANCH_WIDTH = D_MODEL // 4
HEAD_DIM = 128
N_HEADS = BRANCH_WIDTH // HEAD_DIM
MLA_ROPE_DIM = 64
MLA_Q_RANK = (3 * D_MODEL) // 16
MLA_KV_RANK = 512
HYENA_WIDTH = BRANCH_WIDTH
HYENA_BANDS = 16
HYENA_DECAY_TARGET = 1e-2
DIFF_QK_DIM = 64
GRID_W = 64
NA_ROWS_MAX = 8
NA_COLS = 16
N_GROUPS = 4
EXPERTS_PER_GROUP = 8
N_EXPERTS = N_GROUPS * EXPERTS_PER_GROUP
EXPERT_FF = D_MODEL // 8
MASK_VALUE = -1e30

OFF_QD, OFF_KD, OFF_VD = 0, 1024, 2048
OFF_QNA, OFF_KNA, OFF_VNA = 3072, 4096, 5120
OFF_HY = 6144
OFF_CQ = 9216
OFF_KR = 9984
OFF_CKV = 10240
N_PROJ = 10752
OFF_GATE = N_PROJ
N_PACKED = N_PROJ + 4 * D_MODEL

MOE_TM = 256


def _params(semantics, vmem_mb):
    return pltpu.CompilerParams(dimension_semantics=semantics, vmem_limit_bytes=vmem_mb << 20)


def _norm(x, g):
    return x * lax.rsqrt(jnp.mean(x * x, axis=-1, keepdims=True) + NORM_EPS) * g


def _rms_kernel(x_ref, g_ref, h_ref):
    h_ref[...] = _norm(x_ref[...], g_ref[...]).astype(h_ref.dtype)


def rms_norm(x, g, out_dtype, tm=256):
    t, d = x.shape
    return pl.pallas_call(
        _rms_kernel,
        out_shape=jax.ShapeDtypeStruct((t, d), out_dtype),
        grid=(t // tm,),
        in_specs=[pl.BlockSpec((tm, d), lambda i: (i, 0)), pl.BlockSpec((1, d), lambda i: (0, 0))],
        out_specs=pl.BlockSpec((tm, d), lambda i: (i, 0)),
        compiler_params=_params(("parallel",), 40),
        name="rms_norm",
    )(x, g.reshape(1, d))


def _combine_rms_kernel(x_ref, a_ref, b_ref, g_ref, *out_refs):
    x = x_ref[...] + a_ref[...] + b_ref[...]
    if len(out_refs) == 2:
        out_refs[0][...] = x
    out_refs[-1][...] = _norm(x, g_ref[...]).astype(out_refs[-1].dtype)


def combine_rms_norm(x, out2, g, out_dtype, emit_x, tm=256):
    t, d = x.shape
    nb = t // tm
    row = pl.BlockSpec((tm, d), lambda i: (i, 0))
    shapes = [jax.ShapeDtypeStruct((t, d), out_dtype)]
    if emit_x:
        shapes = [jax.ShapeDtypeStruct((t, d), F32)] + shapes
    return pl.pallas_call(
        _combine_rms_kernel,
        out_shape=shapes,
        grid=(nb,),
        in_specs=[row, row, pl.BlockSpec((tm, d), lambda i: (i + nb, 0)), pl.BlockSpec((1, d), lambda i: (0, 0))],
        out_specs=[row] * len(shapes),
        compiler_params=_params(("parallel",), 56),
        name="combine_rms_norm",
    )(x, out2, out2, g.reshape(1, d))


def _mm_kernel(a_ref, w_ref, o_ref):
    o_ref[...] = jnp.dot(a_ref[...], w_ref[...], preferred_element_type=F32).astype(o_ref.dtype)


def _mm_res_kernel(a_ref, w_ref, r_ref, o_ref):
    o_ref[...] = r_ref[...] + jnp.dot(a_ref[...], w_ref[...], preferred_element_type=F32)


def matmul(a, w, n_out, out_dtype, tm, tn, res=None, vmem_mb=48, name="matmul"):
    m, k = a.shape
    in_specs = [pl.BlockSpec((tm, k), lambda i, j: (i, 0)), pl.BlockSpec((k, tn), lambda i, j: (0, j))]
    args = [a, w]
    body = _mm_kernel
    if res is not None:
        in_specs.append(pl.BlockSpec((tm, tn), lambda i, j: (i, j)))
        args.append(res)
        body = _mm_res_kernel
    return pl.pallas_call(
        body,
        out_shape=jax.ShapeDtypeStruct((m, n_out), out_dtype),
        grid=(m // tm, n_out // tn),
        in_specs=in_specs,
        out_specs=pl.BlockSpec((tm, tn), lambda i, j: (i, j)),
        compiler_params=_params(("parallel", "parallel"), vmem_mb),
        name=name,
    )(*args)


def _norm_mm_kernel(a_ref, g_ref, w_ref, o_ref):
    an = _norm(a_ref[...].astype(F32), g_ref[...]).astype(BF16)
    o_ref[...] = jnp.dot(an, w_ref[...], preferred_element_type=F32).astype(o_ref.dtype)


def norm_matmul(a, col_block, k, g, w, out_dtype, tm, vmem_mb=40, name="norm_matmul"):
    m = a.shape[0]
    n = w.shape[1]
    return pl.pallas_call(
        _norm_mm_kernel,
        out_shape=jax.ShapeDtypeStruct((m, n), out_dtype),
        grid=(m // tm,),
        in_specs=[pl.BlockSpec((tm, k), lambda i: (i, col_block)),
                  pl.BlockSpec((1, k), lambda i: (0, 0)),
                  pl.BlockSpec((k, n), lambda i: (0, 0))],
        out_specs=pl.BlockSpec((tm, n), lambda i: (i, 0)),
        compiler_params=_params(("parallel",), vmem_mb),
        name=name,
    )(a, g.reshape(1, k), w)


def _rope(x, cos_t, sin_a, sin_b):
    return x * cos_t + pltpu.roll(x, 96, 1) * sin_a + pltpu.roll(x, 32, 1) * sin_b


def rope_tables(length, chunks):
    half = MLA_ROPE_DIM // 2
    inv = 1.0 / (ROPE_THETA ** (jnp.arange(0, MLA_ROPE_DIM, 2, dtype=F32) / MLA_ROPE_DIM))
    ang = jnp.arange(length, dtype=F32)[:, None] * inv[None, :]
    cos, sin = jnp.cos(ang), jnp.sin(ang)
    zero = jnp.zeros((length, half), F32)
    c, sa, sb = [], [], []
    for i in range(2):
        on = i < chunks
        c += [cos, cos] if on else [zero, zero]
        sa += [-sin, zero] if on else [zero, zero]
        sb += [zero, sin] if on else [zero, zero]
    return jnp.concatenate(c, 1), jnp.concatenate(sa, 1), jnp.concatenate(sb, 1)


LOG2E = math.log2(math.e)


def _softmax_pv(s2, v):
    m = jnp.max(s2, axis=-1, keepdims=True)
    p = jnp.exp2(s2 - m)
    l = jnp.sum(p, axis=-1, keepdims=True)
    return jnp.dot(p.astype(BF16), v, preferred_element_type=F32) / l


def _qk(q, k):
    return lax.dot_general(q, k, (((1,), (1,)), ((), ())), preferred_element_type=F32)


def _mla_kernel(q_ref, kn_ref, v_ref, kr_ref, cq_ref, saq_ref, sbq_ref, ck_ref, sak_ref, sbk_ref, o_ref, *, scale):
    q = q_ref[...]
    c = scale * LOG2E
    qn = q[:, :HEAD_DIM].astype(F32) * c
    qr = _rope(q[:, HEAD_DIM:].astype(F32), cq_ref[...], saq_ref[...], sbq_ref[...]) * c
    qf = jnp.concatenate([qn.astype(BF16), qr.astype(BF16)], axis=1)
    kr = _rope(kr_ref[...].astype(F32), ck_ref[...], sak_ref[...], sbk_ref[...])
    kf = jnp.concatenate([kn_ref[...], kr.astype(BF16)], axis=1)
    o_ref[...] = _softmax_pv(_qk(qf, kf), v_ref[...]).astype(o_ref.dtype)


def mla_attention(qf, kv, proj, tabs, batch, length, tq):
    t = batch * length
    nq = length // tq
    scale = (HEAD_DIM + MLA_ROPE_DIM) ** -0.5
    tab_q = pl.BlockSpec((tq, HEAD_DIM), lambda b, h, i: (i, 0))
    tab_k = pl.BlockSpec((length, HEAD_DIM), lambda b, h, i: (0, 0))
    return pl.pallas_call(
        functools.partial(_mla_kernel, scale=scale),
        out_shape=jax.ShapeDtypeStruct((t, BRANCH_WIDTH), BF16),
        grid=(batch, N_HEADS, nq),
        in_specs=[pl.BlockSpec((tq, 2 * HEAD_DIM), lambda b, h, i: (b * nq + i, h)),
                  pl.BlockSpec((length, HEAD_DIM), lambda b, h, i: (b, 2 * h)),
                  pl.BlockSpec((length, HEAD_DIM), lambda b, h, i: (b, 2 * h + 1)),
                  pl.BlockSpec((length, HEAD_DIM), lambda b, h, i: (b, OFF_KR // HEAD_DIM)),
                  tab_q, tab_q, tab_q, tab_k, tab_k, tab_k],
        out_specs=pl.BlockSpec((tq, HEAD_DIM), lambda b, h, i: (b * nq + i, h)),
        compiler_params=_params(("parallel", "parallel", "parallel"), 48),
        name="mla_attention",
    )(qf, kv, kv, proj, *tabs, *tabs)


def _diff_kernel(q_ref, k_ref, v_ref, cq_ref, saq_ref, sbq_ref, ck_ref, sak_ref, sbk_ref, lam_ref, g_ref, o_ref,
                 *, scale, out_scale):
    q = _rope(q_ref[...].astype(F32), cq_ref[...], saq_ref[...], sbq_ref[...])
    q = q * (scale * LOG2E)
    k = _rope(k_ref[...].astype(F32), ck_ref[...], sak_ref[...], sbk_ref[...]).astype(BF16)
    tq = q.shape[0]
    lane = lax.broadcasted_iota(jnp.int32, q.shape, 1)
    q2 = jnp.concatenate([jnp.where(lane < DIFF_QK_DIM, q, 0.0), jnp.where(lane >= DIFF_QK_DIM, q, 0.0)], axis=0)
    o2 = _softmax_pv(_qk(q2.astype(BF16), k), v_ref[...])
    o = o2[:tq] - lam_ref[...] * o2[tq:]
    o_ref[...] = (_norm(o, g_ref[...]) * out_scale).astype(o_ref.dtype)


def diff_attention(proj, tabs, lam_full, subln_g, lambda_init, batch, length, tq):
    t = batch * length
    nq = length // tq
    tab_q = pl.BlockSpec((tq, HEAD_DIM), lambda b, h, i: (i, 0))
    tab_k = pl.BlockSpec((length, HEAD_DIM), lambda b, h, i: (0, 0))
    vec = pl.BlockSpec((1, HEAD_DIM), lambda b, h, i: (0, 0))
    return pl.pallas_call(
        functools.partial(_diff_kernel, scale=DIFF_QK_DIM ** -0.5, out_scale=1.0 - lambda_init),
        out_shape=jax.ShapeDtypeStruct((t, BRANCH_WIDTH), BF16),
        grid=(batch, N_HEADS, nq),
        in_specs=[pl.BlockSpec((tq, HEAD_DIM), lambda b, h, i: (b * nq + i, OFF_QD // HEAD_DIM + h)),
                  pl.BlockSpec((length, HEAD_DIM), lambda b, h, i: (b, OFF_KD // HEAD_DIM + h)),
                  pl.BlockSpec((length, HEAD_DIM), lambda b, h, i: (b, OFF_VD // HEAD_DIM + h)),
                  tab_q, tab_q, tab_q, tab_k, tab_k, tab_k, vec, vec],
        out_specs=pl.BlockSpec((tq, HEAD_DIM), lambda b, h, i: (b * nq + i, h)),
        compiler_params=_params(("parallel", "parallel", "parallel"), 48),
        name="diff_attention",
    )(proj, proj, proj, *tabs, *tabs, jnp.full((1, HEAD_DIM), lam_full, F32), subln_g.reshape(1, HEAD_DIM))


def _na_kernel(q_ref, k_ref, v_ref, b_ref, o_ref, *, rows, kr, scale):
    r = pl.program_id(1)
    start = jnp.clip(r - kr // 2, 0, rows - kr)
    base = pl.multiple_of(start * GRID_W, GRID_W)
    outs = []
    for h in range(N_HEADS):
        cols = slice(h * HEAD_DIM, (h + 1) * HEAD_DIM)
        qh = (q_ref[:, cols].astype(F32) * (scale * LOG2E)).astype(BF16)
        s2 = _qk(qh, k_ref[pl.ds(base, kr * GRID_W), cols]) + b_ref[0, h]
        outs.append(_softmax_pv(s2, v_ref[pl.ds(base, kr * GRID_W), cols]))
    o_ref[...] = jnp.concatenate(outs, axis=1).astype(o_ref.dtype)


def na_bias_table(rpb, rows):
    kr = min(NA_ROWS_MAX, rows)
    cols = jnp.arange(GRID_W)
    col_start = jnp.clip(cols - NA_COLS // 2, 0, GRID_W - NA_COLS)
    col_in = (cols[None, :] >= col_start[:, None]) & (cols[None, :] < col_start[:, None] + NA_COLS)
    col_idx = jnp.clip(cols[None, :] - cols[:, None], -(NA_COLS - 1), NA_COLS - 1) + NA_COLS - 1
    col_bias = rpb[:, :, col_idx].astype(F32)
    row_start = jnp.clip(jnp.arange(rows) - kr // 2, 0, rows - kr)
    row_idx = row_start[:, None] + jnp.arange(kr)[None, :] - jnp.arange(rows)[:, None] + NA_ROWS_MAX - 1
    bias = jnp.transpose(col_bias[:, row_idx], (1, 0, 3, 2, 4))
    bias = jnp.where(col_in[None, None, :, None, :], bias * LOG2E, MASK_VALUE)
    return bias.reshape(rows, N_HEADS, GRID_W, kr * GRID_W)


def na_attention(proj, bias, batch, length):
    t = batch * length
    rows = length // GRID_W
    kr = min(NA_ROWS_MAX, rows)
    kv_spec = lambda off: pl.BlockSpec((length, BRANCH_WIDTH), lambda b, r: (b, off // BRANCH_WIDTH))
    return pl.pallas_call(
        functools.partial(_na_kernel, rows=rows, kr=kr, scale=HEAD_DIM ** -0.5),
        out_shape=jax.ShapeDtypeStruct((t, BRANCH_WIDTH), BF16),
        grid=(batch, rows),
        in_specs=[pl.BlockSpec((GRID_W, BRANCH_WIDTH), lambda b, r: (b * rows + r, OFF_QNA // BRANCH_WIDTH)),
                  kv_spec(OFF_KNA), kv_spec(OFF_VNA),
                  pl.BlockSpec((1, N_HEADS, GRID_W, kr * GRID_W), lambda b, r: (r, 0, 0, 0))],
        out_specs=pl.BlockSpec((GRID_W, BRANCH_WIDTH), lambda b, r: (b * rows + r, 0)),
        compiler_params=_params(("parallel", "parallel"), 40),
        name="na_attention",
    )(proj, proj, proj, bias)


HY_TC = 512


def _short_conv_kernel(u_ref, w_ref, b_ref, o_ref):
    u = u_ref[...].astype(F32)
    n = u.shape[0]
    row = lax.broadcasted_iota(jnp.int32, u.shape, 0)
    prev = jnp.where(row == 0, 0.0, pltpu.roll(u, 1, 0))
    nxt = jnp.where(row == n - 1, 0.0, pltpu.roll(u, n - 1, 0))
    w = w_ref[...]
    o_ref[...] = (prev * w[0:1] + u * w[1:2] + nxt * w[2:3] + b_ref[...]).astype(o_ref.dtype)


def short_conv(proj, w, bias, batch, length):
    t = batch * length
    width = 3 * HYENA_WIDTH
    return pl.pallas_call(
        _short_conv_kernel,
        out_shape=jax.ShapeDtypeStruct((t, width), BF16),
        grid=(batch, width // HY_TC),
        in_specs=[pl.BlockSpec((length, HY_TC), lambda b, c: (b, OFF_HY // HY_TC + c)),
                  pl.BlockSpec((3, HY_TC), lambda b, c: (0, c)),
                  pl.BlockSpec((1, HY_TC), lambda b, c: (0, c))],
        out_specs=pl.BlockSpec((length, HY_TC), lambda b, c: (b, c)),
        compiler_params=_params(("parallel", "parallel"), 48),
        name="short_conv",
    )(proj, w, bias.reshape(1, width))


def dft_matrices(length):
    n = 2 * length
    kb = min(512, length)
    nkb = length // kb
    k = jnp.arange(length, dtype=jnp.int32)[:, None]
    s = jnp.arange(length, dtype=jnp.int32)[None, :]
    ang = ((k * s) % n).astype(F32) * (2.0 * math.pi / n)
    c, sn = jnp.cos(ang), jnp.sin(ang)
    alt_s = jnp.where(s % 2 == 0, 1.0, -1.0).astype(F32)
    alt_t = jnp.where(k % 2 == 0, 1.0, -1.0).astype(F32)
    f_im = jnp.where(k == 0, alt_s, -sn)
    f_fwd = jnp.stack([c.reshape(nkb, kb, length), f_im.reshape(nkb, kb, length)], axis=1).reshape(n, length)
    g_c = jnp.where(s == 0, 1.0 / n, c * (2.0 / n))
    g_s = jnp.where(s == 0, alt_t / n, -sn * (2.0 / n))
    g_inv = jnp.stack([g_c.reshape(length, nkb, kb), g_s.reshape(length, nkb, kb)], axis=2).reshape(length, n)
    return f_fwd.astype(BF16), g_inv.astype(BF16)


def hyena_filter_spectrum(length, w1, b1, w2, b2, w3, freq, decay, f_fwd):
    hp = lax.Precision.HIGHEST
    pos = jnp.arange(length, dtype=F32)
    tt = (pos / max(length - 1, 1))[:, None]
    bands = jnp.linspace(1e-4, HYENA_BANDS - 1, HYENA_BANDS, dtype=F32)
    ang = (2.0 * math.pi / length) * pos[:, None] * bands[None, :]
    feats = jnp.concatenate([tt, jnp.cos(ang), jnp.sin(ang)], axis=-1)
    h = jnp.sin(freq[0] * (jnp.dot(feats, w1, precision=hp) + b1))
    h = jnp.sin(freq[1] * (jnp.dot(h, w2, precision=hp) + b2))
    h = jnp.dot(h, w3, precision=hp).reshape(length, 2, 2, HYENA_WIDTH)
    h = h * jnp.exp(-tt[:, :, None, None] * jnp.abs(decay))
    width = 2 * HYENA_WIDTH
    fwd = h[:, :, 0].reshape(length, width)
    bwd = jnp.where(pos[:, None] == 0, 0.0, h[:, :, 1].reshape(length, width))
    inv_l1 = 1.0 / (jnp.sum(jnp.abs(fwd), axis=0) + jnp.sum(jnp.abs(bwd), axis=0))
    n = 2 * length
    kb = min(512, length)
    spec = matmul(f_fwd, jnp.concatenate([fwd, bwd], axis=1).astype(BF16), 2 * width, F32,
                  tm=min(1024, n), tn=512, name="filter_dft")
    spec = spec.reshape(length // kb, 2, kb, 2, width)
    k_re = (spec[:, 0, :, 0] + spec[:, 0, :, 1]).reshape(length, width) * inv_l1
    k_im = (spec[:, 1, :, 0] - spec[:, 1, :, 1]).reshape(length, width) * inv_l1
    nyquist = (spec[0, 1, 0, 0] + spec[0, 1, 0, 1]) * inv_l1
    first = jnp.arange(length)[:, None] == 0
    return k_re, jnp.where(first, 0.0, k_im), jnp.where(first, nyquist[None, :], k_re)


def _dft_fwd_kernel(f_ref, z_ref, a_ref, b_ref, a2_ref, o_ref):
    res = jnp.dot(f_ref[...], z_ref[...], preferred_element_type=F32)
    kb = res.shape[0] // 2
    z_re, z_im = res[:kb], res[kb:]
    b = b_ref[...]
    o_ref[:kb, :] = (z_re * a_ref[...] - z_im * b).astype(o_ref.dtype)
    o_ref[kb:, :] = (z_re * b + z_im * a2_ref[...]).astype(o_ref.dtype)


def dft_forward(f_fwd, z, z_col, coefs, order, batch, length):
    n = 2 * length
    kb = min(512, length)
    nkb = length // kb
    nc = HYENA_WIDTH // HY_TC
    coef = pl.BlockSpec((kb, HY_TC), lambda b, c, k: (k, order * nc + c))
    return pl.pallas_call(
        _dft_fwd_kernel,
        out_shape=jax.ShapeDtypeStruct((batch * n, HYENA_WIDTH), BF16),
        grid=(batch, nc, nkb),
        in_specs=[pl.BlockSpec((2 * kb, length), lambda b, c, k: (k, 0)),
                  pl.BlockSpec((length, HY_TC), lambda b, c, k: (b, z_col + c)),
                  coef, coef, coef],
        out_specs=pl.BlockSpec((2 * kb, HY_TC), lambda b, c, k: (b * nkb + k, c)),
        compiler_params=_params(("parallel", "parallel", "parallel"), 40),
        name="dft_forward",
    )(f_fwd, z, *coefs)


def _dft_inv_kernel(g_ref, y_ref, x_ref, z_ref, bias_ref, o_ref):
    y = jnp.dot(g_ref[...], y_ref[...], preferred_element_type=F32)
    o_ref[...] = (x_ref[...].astype(F32) * (y + bias_ref[...] * z_ref[...].astype(F32))).astype(o_ref.dtype)


def dft_inverse(g_inv, spec, gate, gate_col, z, z_col, bias, batch, length):
    n = 2 * length
    tt = min(512, length)
    nt = length // tt
    nc = HYENA_WIDTH // HY_TC
    return pl.pallas_call(
        _dft_inv_kernel,
        out_shape=jax.ShapeDtypeStruct((batch * length, HYENA_WIDTH), BF16),
        grid=(batch, nc, nt),
        in_specs=[pl.BlockSpec((tt, n), lambda b, c, i: (i, 0)),
                  pl.BlockSpec((n, HY_TC), lambda b, c, i: (b, c)),
                  pl.BlockSpec((tt, HY_TC), lambda b, c, i: (b * nt + i, gate_col + c)),
                  pl.BlockSpec((tt, HY_TC), lambda b, c, i: (b * nt + i, z_col + c)),
                  pl.BlockSpec((1, HY_TC), lambda b, c, i: (0, c))],
        out_specs=pl.BlockSpec((tt, HY_TC), lambda b, c, i: (b * nt + i, c)),
        compiler_params=_params(("parallel", "parallel", "parallel"), 40),
        name="dft_inverse",
    )(g_inv, spec, gate, z, bias.reshape(1, HYENA_WIDTH))


def hyena_mixer(proj, conv_w, conv_b, coefs, bias, f_fwd, g_inv, batch, length):
    nc = HYENA_WIDTH // HY_TC
    cv = short_conv(proj, conv_w, conv_b, batch, length)
    spec = dft_forward(f_fwd, cv, 0, coefs, 0, batch, length)
    z1 = dft_inverse(g_inv, spec, cv, nc, cv, 0, bias[0], batch, length)
    spec = dft_forward(f_fwd, z1, 0, coefs, 1, batch, length)
    return dft_inverse(g_inv, spec, cv, 2 * nc, z1, 0, bias[1], batch, length)


def _merge_kernel(h_ref, g0, g1, g2, g3, y0, y1, y2, y3, wb_ref, o_ref):
    h = h_ref[...]
    acc = None
    for i, (g_ref, y_ref) in enumerate(((g0, y0), (g1, y1), (g2, y2), (g3, y3))):
        gate = jax.nn.sigmoid(jnp.dot(h, g_ref[...], preferred_element_type=F32))
        term = gate * jnp.dot(y_ref[...], wb_ref[i], preferred_element_type=F32)
        acc = term if acc is None else acc + term
    o_ref[...] = acc.astype(o_ref.dtype)


def gated_merge(h, w_packed, ys, w_branch, tm=512, tn=256):
    t, d = h.shape
    gate_spec = lambda i: pl.BlockSpec((d, tn), lambda n, m: (0, (OFF_GATE + i * d) // tn + n))
    y_spec = pl.BlockSpec((tm, BRANCH_WIDTH), lambda n, m: (m, 0))
    return pl.pallas_call(
        _merge_kernel,
        out_shape=jax.ShapeDtypeStruct((t, d), BF16),
        grid=(d // tn, t // tm),
        in_specs=[pl.BlockSpec((tm, d), lambda n, m: (m, 0))] + [gate_spec(i) for i in range(4)] + [y_spec] * 4
                 + [pl.BlockSpec((4, BRANCH_WIDTH, tn), lambda n, m: (0, 0, n))],
        out_specs=pl.BlockSpec((tm, tn), lambda n, m: (m, n)),
        compiler_params=_params(("parallel", "parallel"), 52),
        name="gated_merge",
    )(h, w_packed, w_packed, w_packed, w_packed, *ys, w_branch)


def moe_routing(logits, rg_b, re_b, tm, n_tiles):
    t = logits.shape[0]
    g_prob = jax.nn.softmax(logits[:, :N_GROUPS] + rg_b, axis=-1)
    g_idx = jnp.argmax(g_prob, axis=-1)
    g_val = jnp.max(g_prob, axis=-1)
    e_logit = (logits[:, N_GROUPS:N_GROUPS + N_EXPERTS] + re_b).reshape(t, N_GROUPS, EXPERTS_PER_GROUP)
    e_sel = jnp.take_along_axis(e_logit, g_idx[:, None, None], axis=1)[:, 0]
    top_val, top_idx = lax.top_k(jax.nn.softmax(e_sel, axis=-1), 2)
    top_val = top_val / jnp.sum(top_val, axis=-1, keepdims=True)
    weight = (g_val[:, None] * top_val).T.reshape(-1)
    expert = (g_idx[:, None] * EXPERTS_PER_GROUP + top_idx).T.reshape(-1).astype(jnp.int32)

    order = jnp.argsort(expert, stable=True).astype(jnp.int32)
    counts = jnp.sum(expert[None, :] == jnp.arange(N_EXPERTS, dtype=jnp.int32)[:, None], axis=1, dtype=jnp.int32)
    padded = ((counts + tm - 1) // tm) * tm
    pad_end = jnp.cumsum(padded)
    pad_start = pad_end - padded
    start = jnp.cumsum(counts) - counts
    tile_start = jnp.arange(n_tiles, dtype=jnp.int32) * tm
    tile_expert = jnp.minimum(jnp.sum(pad_end[None, :] <= tile_start[:, None], axis=1, dtype=jnp.int32), N_EXPERTS - 1)
    n_used = (pad_end[-1] // tm).astype(jnp.int32).reshape(1)
    row_expert = jnp.repeat(tile_expert, tm)
    rows = jnp.arange(n_tiles * tm, dtype=jnp.int32)
    off = rows - pad_start[row_expert]
    valid = (off < counts[row_expert]) & (rows < pad_end[-1])
    src = order[jnp.clip(start[row_expert] + off, 0, 2 * t - 1)]
    row_tok = jnp.where(valid, src % t, 0).astype(jnp.int32)
    row_dst = jnp.where(valid, src, 2 * t + rows % tm).astype(jnp.int32)
    row_w = jnp.where(valid, weight[src], 0.0).astype(F32)
    return tile_expert, n_used, row_tok, row_dst, row_w


MOE_UNROLL = 8


def _moe_kernel(te_ref, nu_ref, tok_ref, tok_next_ref, dst_ref, w_ref, x_hbm, g_ref, wg_ref, wu_ref, wd_ref, o_hbm,
                xbuf, obuf, gsem, ssem):
    tm = xbuf.shape[1]
    i = pl.program_id(0)
    n_used = nu_ref[0]
    slot = i % 2

    def issue_gather(idx_ref, s):
        @pl.loop(0, tm // MOE_UNROLL)
        def _(c):
            for j in range(MOE_UNROLL):
                r = c * MOE_UNROLL + j
                pltpu.make_async_copy(x_hbm.at[pl.ds(idx_ref[0, 0, r], 1)], xbuf.at[s, pl.ds(r, 1)],
                                      gsem.at[s]).start()

    def wait_gather(s):
        pltpu.make_async_copy(x_hbm.at[pl.ds(0, tm)], xbuf.at[s], gsem.at[s]).wait()

    def wait_scatter(s):
        pltpu.make_async_copy(obuf.at[s], o_hbm.at[pl.ds(0, tm)], ssem.at[s]).wait()

    @pl.when(i < n_used)
    def _():
        @pl.when(i == 0)
        def _():
            issue_gather(tok_ref, slot)
            obuf[1] = jnp.zeros(obuf.shape[1:], obuf.dtype)
            init = pltpu.make_async_copy(obuf.at[1], o_hbm.at[pl.ds(o_hbm.shape[0] - tm, tm)], ssem.at[1])
            init.start()
            init.wait()

        wait_gather(slot)

        @pl.when(i + 1 < n_used)
        def _():
            issue_gather(tok_next_ref, 1 - slot)

        xn = _norm(xbuf[slot], g_ref[...]).astype(BF16)
        a = jnp.dot(xn, wg_ref[0], preferred_element_type=F32)
        u = jnp.dot(xn, wu_ref[0], preferred_element_type=F32)
        hid = (a * jax.nn.sigmoid(a) * u * w_ref[...]).astype(BF16)
        res = jnp.dot(hid, wd_ref[0], preferred_element_type=F32)

        @pl.when(i >= 2)
        def _():
            wait_scatter(slot)

        obuf[slot] = res

        @pl.loop(0, tm // MOE_UNROLL)
        def _(c):
            for j in range(MOE_UNROLL):
                r = c * MOE_UNROLL + j
                pltpu.make_async_copy(obuf.at[slot, pl.ds(r, 1)], o_hbm.at[pl.ds(dst_ref[0, 0, r], 1)],
                                      ssem.at[slot]).start()

        @pl.when(i == n_used - 1)
        def _():
            wait_scatter(slot)

            @pl.when(i >= 1)
            def _():
                wait_scatter(1 - slot)


def moe_experts(x, g, schedule, w_gate, w_up, w_down, layer, tm):
    t, d = x.shape
    tile_expert, n_used, row_tok, row_dst, row_w = schedule
    n_tiles = tile_expert.shape[0]
    w_idx = lambda i, te, nu: (layer * N_EXPERTS + te[i], 0, 0)
    idx_spec = lambda off: pl.BlockSpec((1, 1, tm), lambda i, te, nu: (jnp.minimum(i + off, n_tiles - 1), 0, 0),
                                        memory_space=pltpu.SMEM)
    grid_spec = pltpu.PrefetchScalarGridSpec(
        num_scalar_prefetch=2,
        grid=(n_tiles,),
        in_specs=[idx_spec(0), idx_spec(1), idx_spec(0),
                  pl.BlockSpec((tm, 1), lambda i, te, nu: (i, 0)),
                  pl.BlockSpec(memory_space=pl.ANY),
                  pl.BlockSpec((1, d), lambda i, te, nu: (0, 0)),
                  pl.BlockSpec((1, d, EXPERT_FF), w_idx),
                  pl.BlockSpec((1, d, EXPERT_FF), w_idx),
                  pl.BlockSpec((1, EXPERT_FF, d), w_idx)],
        out_specs=pl.BlockSpec(memory_space=pl.ANY),
        scratch_shapes=[pltpu.VMEM((2, tm, d), F32), pltpu.VMEM((2, tm, d), F32),
                        pltpu.SemaphoreType.DMA((2,)), pltpu.SemaphoreType.DMA((2,))],
    )
    tok3 = row_tok.reshape(n_tiles, 1, tm)
    return pl.pallas_call(
        _moe_kernel,
        out_shape=jax.ShapeDtypeStruct((2 * t + tm, d), F32),
        grid_spec=grid_spec,
        compiler_params=_params(("arbitrary",), 56),
        name="moe_experts",
    )(tile_expert, n_used, tok3, tok3, row_dst.reshape(n_tiles, 1, tm),
      row_w.reshape(n_tiles * tm, 1), x, g.reshape(1, d), w_gate, w_up, w_down)


def pack_w_in(w):
    d = w.shape[0]
    s = [0, 768, 1280, 1344, 4416, 5440, 6464, 7488, 8512, 9536, 10560]
    c_q, c_kv, k_rope, u_hy = w[:, s[0]:s[1]], w[:, s[1]:s[2]], w[:, s[2]:s[3]], w[:, s[3]:s[4]]
    z = lambda n: jnp.zeros((d, n), w.dtype)
    return jnp.concatenate([w[:, s[4]:s[10]], u_hy, c_q, k_rope, z(64), z(128), c_kv, w[:, s[10]:]],
                           axis=1).astype(BF16)


def pack_w_uq(w):
    r = w.shape[0]
    w = w.reshape(r, N_HEADS, HEAD_DIM + MLA_ROPE_DIM)
    w = jnp.concatenate([w, jnp.zeros((r, N_HEADS, HEAD_DIM - MLA_ROPE_DIM), w.dtype)], axis=-1)
    return w.reshape(r, N_HEADS * 2 * HEAD_DIM).astype(BF16)


def pack_router(rg_w, re_w):
    d = rg_w.shape[0]
    pad = jnp.zeros((d, HEAD_DIM - N_GROUPS - N_EXPERTS), rg_w.dtype)
    return jnp.concatenate([rg_w, re_w, pad], axis=1).astype(BF16)


def kernel(x, norm_mix_g, w_in, mla_q_norm_g, mla_kv_norm_g, mla_w_uq, mla_w_ukv, hyena_conv_w, hyena_conv_b, hyena_ffn_w1, hyena_ffn_b1, hyena_ffn_w2, hyena_ffn_b2, hyena_ffn_w3, hyena_sin_freq, hyena_decay, hyena_bias, diff_lambda, diff_subln_g, na_rpb, w_branch, w_out, norm_ffn_g, router_group_w, router_group_b, router_expert_w, router_expert_b, moe_w_gate, moe_w_up, moe_w_down, norm_final_g):
    batch, length, d = x.shape
    t = batch * length
    assert d == D_MODEL and length % GRID_W == 0 and t % 1024 == 0
    tq = min(512, length)
    n_tiles = 2 * t // MOE_TM + N_EXPERTS

    rope_mla = rope_tables(length, 1)
    rope_diff = rope_tables(length, 2)
    f_fwd, g_inv = dft_matrices(length)
    moe_wg = moe_w_gate.astype(BF16).reshape(DEPTH * N_EXPERTS, d, EXPERT_FF)
    moe_wu = moe_w_up.astype(BF16).reshape(DEPTH * N_EXPERTS, d, EXPERT_FF)
    moe_wd = moe_w_down.astype(BF16).reshape(DEPTH * N_EXPERTS, EXPERT_FF, d)

    x2 = x.reshape(t, d)
    h = rms_norm(x2, norm_mix_g[0], BF16)
    out = None
    for l in range(DEPTH):
        w_packed = pack_w_in(w_in[l])
        proj = matmul(h, w_packed, N_PROJ, BF16, tm=1024, tn=512, name="in_proj")

        qf = norm_matmul(proj, OFF_CQ // MLA_Q_RANK, MLA_Q_RANK, mla_q_norm_g[l], pack_w_uq(mla_w_uq[l]), BF16,
                         tm=512, name="mla_q_up")
        kv = norm_matmul(proj, OFF_CKV // MLA_KV_RANK, MLA_KV_RANK, mla_kv_norm_g[l], mla_w_ukv[l].astype(BF16), BF16,
                         tm=512, name="mla_kv_up")
        y_a = mla_attention(qf, kv, proj, rope_mla, batch, length, tq)

        coefs = hyena_filter_spectrum(length, hyena_ffn_w1[l], hyena_ffn_b1[l], hyena_ffn_w2[l], hyena_ffn_b2[l],
                                      hyena_ffn_w3[l], hyena_sin_freq[l], hyena_decay[l], f_fwd)
        y_b = hyena_mixer(proj, hyena_conv_w[l], hyena_conv_b[l], coefs, hyena_bias[l], f_fwd, g_inv, batch, length)

        lambda_init = 0.8 - 0.6 * math.exp(-0.3 * l)
        lam = diff_lambda[l].astype(F32)
        lam_full = jnp.exp(jnp.sum(lam[0] * lam[1])) - jnp.exp(jnp.sum(lam[2] * lam[3])) + lambda_init
        y_c = diff_attention(proj, rope_diff, lam_full, diff_subln_g[l], lambda_init, batch, length, tq)

        y_d = na_attention(proj, na_bias_table(na_rpb[l], length // GRID_W), batch, length)

        merged = gated_merge(h, w_packed, (y_a, y_b, y_c, y_d), w_branch[l].astype(BF16))
        x_mid = matmul(merged, w_out[l].astype(BF16), d, F32, tm=1024, tn=512, res=x2, name="out_proj")

        logits = norm_matmul(x_mid, 0, d, norm_ffn_g[l], pack_router(router_group_w[l], router_expert_w[l]), F32,
                             tm=256, name="router")
        schedule = moe_routing(logits, router_group_b[l], router_expert_b[l], MOE_TM, n_tiles)
        out2 = moe_experts(x_mid, norm_ffn_g[l], schedule, moe_wg, moe_wu, moe_wd, l, MOE_TM)
        if l + 1 < DEPTH:
            x2, h = combine_rms_norm(x_mid, out2, norm_mix_g[l + 1], BF16, emit_x=True)
        else:
            out = combine_rms_norm(x_mid, out2, norm_final_g, F32, emit_x=False)[0]
    return out.reshape(batch, length, d)
```

```python
import functools
import math

import jax
import jax.numpy as jnp
from jax import lax
from jax.experimental import pallas as pl
from jax.experimental.pallas import tpu as pltpu

F32 = jnp.float32
BF16 = jnp.bfloat16

D_MODEL = 4096
DEPTH = 2
NORM_EPS = 1e-6
ROPE_THETA = 10000.0
BRANCH_WIDTH = D_MODEL // 4
HEAD_DIM = 128
N_HEADS = BRANCH_WIDTH // HEAD_DIM
MLA_ROPE_DIM = 64
MLA_Q_RANK = (3 * D_MODEL) // 16
MLA_KV_RANK = 512
HYENA_WIDTH = BRANCH_WIDTH
HYENA_BANDS = 16
HYENA_DECAY_TARGET = 1e-2
DIFF_QK_DIM = 64
GRID_W = 64
NA_ROWS_MAX = 8
NA_COLS = 16
N_GROUPS = 4
EXPERTS_PER_GROUP = 8
N_EXPERTS = N_GROUPS * EXPERTS_PER_GROUP
EXPERT_FF = D_MODEL // 8
MASK_VALUE = -1e30

OFF_QD, OFF_KD, OFF_VD = 0, 1024, 2048
OFF_QNA, OFF_KNA, OFF_VNA = 3072, 4096, 5120
OFF_HY = 6144
OFF_CQ = 9216
OFF_KR = 9984
OFF_CKV = 10240
N_PROJ = 10752
OFF_GATE = N_PROJ
N_PACKED = N_PROJ + 4 * D_MODEL

MOE_TM = 256


def _params(semantics, vmem_mb):
    return pltpu.CompilerParams(dimension_semantics=semantics, vmem_limit_bytes=vmem_mb << 20)


def _norm(x, g):
    return x * lax.rsqrt(jnp.mean(x * x, axis=-1, keepdims=True) + NORM_EPS) * g


def _rms_kernel(x_ref, g_ref, h_ref):
    h_ref[...] = _norm(x_ref[...], g_ref[...]).astype(h_ref.dtype)


def rms_norm(x, g, out_dtype, tm=256):
    t, d = x.shape
    return pl.pallas_call(
        _rms_kernel,
        out_shape=jax.ShapeDtypeStruct((t, d), out_dtype),
        grid=(t // tm,),
        in_specs=[pl.BlockSpec((tm, d), lambda i: (i, 0)), pl.BlockSpec((1, d), lambda i: (0, 0))],
        out_specs=pl.BlockSpec((tm, d), lambda i: (i, 0)),
        compiler_params=_params(("parallel",), 40),
        name="rms_norm",
    )(x, g.reshape(1, d))


def _combine_rms_kernel(x_ref, a_ref, b_ref, g_ref, *out_refs):
    x = x_ref[...] + a_ref[...] + b_ref[...]
    if len(out_refs) == 2:
        out_refs[0][...] = x
    out_refs[-1][...] = _norm(x, g_ref[...]).astype(out_refs[-1].dtype)


def combine_rms_norm(x, out2, g, out_dtype, emit_x, tm=256):
    t, d = x.shape
    nb = t // tm
    row = pl.BlockSpec((tm, d), lambda i: (i, 0))
    shapes = [jax.ShapeDtypeStruct((t, d), out_dtype)]
    if emit_x:
        shapes = [jax.ShapeDtypeStruct((t, d), F32)] + shapes
    return pl.pallas_call(
        _combine_rms_kernel,
        out_shape=shapes,
        grid=(nb,),
        in_specs=[row, row, pl.BlockSpec((tm, d), lambda i: (i + nb, 0)), pl.BlockSpec((1, d), lambda i: (0, 0))],
        out_specs=[row] * len(shapes),
        compiler_params=_params(("parallel",), 56),
        name="combine_rms_norm",
    )(x, out2, out2, g.reshape(1, d))


def _mm_kernel(a_ref, w_ref, o_ref):
    o_ref[...] = jnp.dot(a_ref[...], w_ref[...], preferred_element_type=F32).astype(o_ref.dtype)


def _mm_res_kernel(a_ref, w_ref, r_ref, o_ref):
    o_ref[...] = r_ref[...] + jnp.dot(a_ref[...], w_ref[...], preferred_element_type=F32)


def matmul(a, w, n_out, out_dtype, tm, tn, res=None, vmem_mb=48, name="matmul"):
    m, k = a.shape
    in_specs = [pl.BlockSpec((tm, k), lambda i, j: (i, 0)), pl.BlockSpec((k, tn), lambda i, j: (0, j))]
    args = [a, w]
    body = _mm_kernel
    if res is not None:
        in_specs.append(pl.BlockSpec((tm, tn), lambda i, j: (i, j)))
        args.append(res)
        body = _mm_res_kernel
    return pl.pallas_call(
        body,
        out_shape=jax.ShapeDtypeStruct((m, n_out), out_dtype),
        grid=(m // tm, n_out // tn),
        in_specs=in_specs,
        out_specs=pl.BlockSpec((tm, tn), lambda i, j: (i, j)),
        compiler_params=_params(("parallel", "parallel"), vmem_mb),
        name=name,
    )(*args)


def _norm_mm_kernel(a_ref, g_ref, w_ref, o_ref):
    an = _norm(a_ref[...].astype(F32), g_ref[...]).astype(BF16)
    o_ref[...] = jnp.dot(an, w_ref[...], preferred_element_type=F32).astype(o_ref.dtype)


def norm_matmul(a, col_block, k, g, w, out_dtype, tm, vmem_mb=40, name="norm_matmul"):
    m = a.shape[0]
    n = w.shape[1]
    return pl.pallas_call(
        _norm_mm_kernel,
        out_shape=jax.ShapeDtypeStruct((m, n), out_dtype),
        grid=(m // tm,),
        in_specs=[pl.BlockSpec((tm, k), lambda i: (i, col_block)),
                  pl.BlockSpec((1, k), lambda i: (0, 0)),
                  pl.BlockSpec((k, n), lambda i: (0, 0))],
        out_specs=pl.BlockSpec((tm, n), lambda i: (i, 0)),
        compiler_params=_params(("parallel",), vmem_mb),
        name=name,
    )(a, g.reshape(1, k), w)


def _rope(x, cos_t, sin_a, sin_b):
    return x * cos_t + pltpu.roll(x, 96, 1) * sin_a + pltpu.roll(x, 32, 1) * sin_b


def rope_tables(length, chunks):
    half = MLA_ROPE_DIM // 2
    inv = 1.0 / (ROPE_THETA ** (jnp.arange(0, MLA_ROPE_DIM, 2, dtype=F32) / MLA_ROPE_DIM))
    ang = jnp.arange(length, dtype=F32)[:, None] * inv[None, :]
    cos, sin = jnp.cos(ang), jnp.sin(ang)
    zero = jnp.zeros((length, half), F32)
    c, sa, sb = [], [], []
    for i in range(2):
        on = i < chunks
        c += [cos, cos] if on else [zero, zero]
        sa += [-sin, zero] if on else [zero, zero]
        sb += [zero, sin] if on else [zero, zero]
    return jnp.concatenate(c, 1), jnp.concatenate(sa, 1), jnp.concatenate(sb, 1)


LOG2E = math.log2(math.e)


def _softmax_pv(s2, v):
    m = jnp.max(s2, axis=-1, keepdims=True)
    p = jnp.exp2(s2 - m)
    l = jnp.sum(p, axis=-1, keepdims=True)
    return jnp.dot(p.astype(BF16), v, preferred_element_type=F32) / l


def _qk(q, k):
    return lax.dot_general(q, k, (((1,), (1,)), ((), ())), preferred_element_type=F32)


def _mla_kernel(q_ref, kn_ref, v_ref, kr_ref, cq_ref, saq_ref, sbq_ref, ck_ref, sak_ref, sbk_ref, o_ref, *, scale):
    q = q_ref[...]
    c = scale * LOG2E
    qn = q[:, :HEAD_DIM].astype(F32) * c
    qr = _rope(q[:, HEAD_DIM:].astype(F32), cq_ref[...], saq_ref[...], sbq_ref[...]) * c
    qf = jnp.concatenate([qn.astype(BF16), qr.astype(BF16)], axis=1)
    kr = _rope(kr_ref[...].astype(F32), ck_ref[...], sak_ref[...], sbk_ref[...])
    kf = jnp.concatenate([kn_ref[...], kr.astype(BF16)], axis=1)
    v = v_ref[...]
    part = qf.shape[0] // 2
    for r in range(2):
        rows = slice(r * part, (r + 1) * part)
        o_ref[rows, :] = _softmax_pv(_qk(qf[rows], kf), v).astype(o_ref.dtype)


def mla_attention(qf, kv, proj, tabs, batch, length, tq):
    t = batch * length
    nq = length // tq
    scale = (HEAD_DIM + MLA_ROPE_DIM) ** -0.5
    tab_q = pl.BlockSpec((tq, HEAD_DIM), lambda b, h, i: (i, 0))
    tab_k = pl.BlockSpec((length, HEAD_DIM), lambda b, h, i: (0, 0))
    return pl.pallas_call(
        functools.partial(_mla_kernel, scale=scale),
        out_shape=jax.ShapeDtypeStruct((t, BRANCH_WIDTH), BF16),
        grid=(batch, N_HEADS, nq),
        in_specs=[pl.BlockSpec((tq, 2 * HEAD_DIM), lambda b, h, i: (b * nq + i, h)),
                  pl.BlockSpec((length, HEAD_DIM), lambda b, h, i: (b, 2 * h)),
                  pl.BlockSpec((length, HEAD_DIM), lambda b, h, i: (b, 2 * h + 1)),
                  pl.BlockSpec((length, HEAD_DIM), lambda b, h, i: (b, OFF_KR // HEAD_DIM)),
                  tab_q, tab_q, tab_q, tab_k, tab_k, tab_k],
        out_specs=pl.BlockSpec((tq, HEAD_DIM), lambda b, h, i: (b * nq + i, h)),
        compiler_params=_params(("parallel", "parallel", "parallel"), 48),
        name="mla_attention",
    )(qf, kv, kv, proj, *tabs, *tabs)


def _diff_kernel(q_ref, k_ref, v_ref, cq_ref, saq_ref, sbq_ref, ck_ref, sak_ref, sbk_ref, lam_ref, g_ref, o_ref,
                 *, scale, out_scale):
    q = _rope(q_ref[...].astype(F32), cq_ref[...], saq_ref[...], sbq_ref[...])
    q = q * (scale * LOG2E)
    k = _rope(k_ref[...].astype(F32), ck_ref[...], sak_ref[...], sbk_ref[...]).astype(BF16)
    lane = lax.broadcasted_iota(jnp.int32, q.shape, 1)
    q0 = jnp.where(lane < DIFF_QK_DIM, q, 0.0).astype(BF16)
    q1 = jnp.where(lane >= DIFF_QK_DIM, q, 0.0).astype(BF16)
    v = v_ref[...]
    half = q.shape[0] // 2
    for r in range(2):
        rows = slice(r * half, (r + 1) * half)
        o = _softmax_pv(_qk(q0[rows], k), v) - lam_ref[...] * _softmax_pv(_qk(q1[rows], k), v)
        o_ref[rows, :] = (_norm(o, g_ref[...]) * out_scale).astype(o_ref.dtype)


def diff_attention(proj, tabs, lam_full, subln_g, lambda_init, batch, length, tq):
    t = batch * length
    nq = length // tq
    tab_q = pl.BlockSpec((tq, HEAD_DIM), lambda b, h, i: (i, 0))
    tab_k = pl.BlockSpec((length, HEAD_DIM), lambda b, h, i: (0, 0))
    vec = pl.BlockSpec((1, HEAD_DIM), lambda b, h, i: (0, 0))
    return pl.pallas_call(
        functools.partial(_diff_kernel, scale=DIFF_QK_DIM ** -0.5, out_scale=1.0 - lambda_init),
        out_shape=jax.ShapeDtypeStruct((t, BRANCH_WIDTH), BF16),
        grid=(batch, N_HEADS, nq),
        in_specs=[pl.BlockSpec((tq, HEAD_DIM), lambda b, h, i: (b * nq + i, OFF_QD // HEAD_DIM + h)),
                  pl.BlockSpec((length, HEAD_DIM), lambda b, h, i: (b, OFF_KD // HEAD_DIM + h)),
                  pl.BlockSpec((length, HEAD_DIM), lambda b, h, i: (b, OFF_VD // HEAD_DIM + h)),
                  tab_q, tab_q, tab_q, tab_k, tab_k, tab_k, vec, vec],
        out_specs=pl.BlockSpec((tq, HEAD_DIM), lambda b, h, i: (b * nq + i, h)),
        compiler_params=_params(("parallel", "parallel", "parallel"), 48),
        name="diff_attention",
    )(proj, proj, proj, *tabs, *tabs, jnp.full((1, HEAD_DIM), lam_full, F32), subln_g.reshape(1, HEAD_DIM))


NA_GROUP = 4


def _na_kernel(q_ref, k_ref, v_ref, cb_ref, o_ref, *, rows, kr, win, scale):
    g = pl.program_id(1)
    w_start = jnp.clip(g * NA_GROUP - kr // 2, 0, rows - win)
    base = pl.multiple_of(w_start * GRID_W, GRID_W)
    nk = win * GRID_W
    left = lax.broadcasted_iota(jnp.int32, (GRID_W, 2 * GRID_W), 1) < GRID_W

    slab, row_mask = [], []
    for j in range(NA_GROUP):
        r = g * NA_GROUP + j
        r_start = jnp.clip(r - kr // 2, 0, rows - kr)
        slab_j, mask_j = [], []
        for p in range(win // 2):
            a = w_start + 2 * p
            ok_l = (a >= r_start) & (a < r_start + kr)
            ok_r = (a + 1 >= r_start) & (a + 1 < r_start + kr)
            slab_j.append(jnp.clip(a - r + NA_ROWS_MAX, 0, 2 * NA_ROWS_MAX - 1))
            mask_j.append(jnp.where(left, jnp.where(ok_l, 0.0, MASK_VALUE), jnp.where(ok_r, 0.0, MASK_VALUE)))
        slab.append(slab_j)
        row_mask.append(jnp.concatenate(mask_j, axis=1))

    outs = []
    for h in range(N_HEADS):
        cols = slice(h * HEAD_DIM, (h + 1) * HEAD_DIM)
        qh = (q_ref[:, cols].astype(F32) * (scale * LOG2E)).astype(BF16)
        bias = jnp.concatenate(
            [jnp.concatenate([cb_ref[h, slab[j][p]] for p in range(win // 2)], axis=1) + row_mask[j]
             for j in range(NA_GROUP)], axis=0)
        s2 = _qk(qh, k_ref[pl.ds(base, nk), cols]) + bias
        outs.append(_softmax_pv(s2, v_ref[pl.ds(base, nk), cols]))
    o_ref[...] = jnp.concatenate(outs, axis=1).astype(o_ref.dtype)


def na_bias_slabs(rpb):
    cols = jnp.arange(GRID_W)
    col_start = jnp.clip(cols - NA_COLS // 2, 0, GRID_W - NA_COLS)
    col_in = (cols[None, :] >= col_start[:, None]) & (cols[None, :] < col_start[:, None] + NA_COLS)
    col_idx = jnp.clip(cols[None, :] - cols[:, None], -(NA_COLS - 1), NA_COLS - 1) + NA_COLS - 1
    col_bias = jnp.where(col_in[None, None], rpb[:, :, col_idx].astype(F32) * LOG2E, MASK_VALUE)
    pad = jnp.full((N_HEADS, 1, GRID_W, GRID_W), MASK_VALUE, F32)
    ext = jnp.concatenate([pad, col_bias, pad], axis=1)
    return jnp.concatenate([ext[:, :-1], ext[:, 1:]], axis=-1)


def na_attention(proj, slabs, batch, length):
    t = batch * length
    rows = length // GRID_W
    kr = min(NA_ROWS_MAX, rows)
    win = min(rows, kr + NA_GROUP)
    assert win % 2 == 0 and rows % NA_GROUP == 0
    steps = rows // NA_GROUP
    kv_spec = lambda off: pl.BlockSpec((length, BRANCH_WIDTH), lambda b, r: (b, off // BRANCH_WIDTH))
    return pl.pallas_call(
        functools.partial(_na_kernel, rows=rows, kr=kr, win=win, scale=HEAD_DIM ** -0.5),
        out_shape=jax.ShapeDtypeStruct((t, BRANCH_WIDTH), BF16),
        grid=(batch, steps),
        in_specs=[pl.BlockSpec((NA_GROUP * GRID_W, BRANCH_WIDTH),
                               lambda b, r: (b * steps + r, OFF_QNA // BRANCH_WIDTH)),
                  kv_spec(OFF_KNA), kv_spec(OFF_VNA),
                  pl.BlockSpec((N_HEADS, 2 * NA_ROWS_MAX, GRID_W, 2 * GRID_W), lambda b, r: (0, 0, 0, 0))],
        out_specs=pl.BlockSpec((NA_GROUP * GRID_W, BRANCH_WIDTH), lambda b, r: (b * steps + r, 0)),
        compiler_params=_params(("parallel", "parallel"), 48),
        name="na_attention",
    )(proj, proj, proj, slabs)


HY_TC = 512


def _short_conv_kernel(u_ref, w_ref, b_ref, o_ref):
    u = u_ref[...].astype(F32)
    n = u.shape[0]
    row = lax.broadcasted_iota(jnp.int32, u.shape, 0)
    prev = jnp.where(row == 0, 0.0, pltpu.roll(u, 1, 0))
    nxt = jnp.where(row == n - 1, 0.0, pltpu.roll(u, n - 1, 0))
    w = w_ref[...]
    o_ref[...] = (prev * w[0:1] + u * w[1:2] + nxt * w[2:3] + b_ref[...]).astype(o_ref.dtype)


def short_conv(proj, w, bias, batch, length):
    t = batch * length
    width = 3 * HYENA_WIDTH
    return pl.pallas_call(
        _short_conv_kernel,
        out_shape=jax.ShapeDtypeStruct((t, width), BF16),
        grid=(batch, width // HY_TC),
        in_specs=[pl.BlockSpec((length, HY_TC), lambda b, c: (b, OFF_HY // HY_TC + c)),
                  pl.BlockSpec((3, HY_TC), lambda b, c: (0, c)),
                  pl.BlockSpec((1, HY_TC), lambda b, c: (0, c))],
        out_specs=pl.BlockSpec((length, HY_TC), lambda b, c: (b, c)),
        compiler_params=_params(("parallel", "parallel"), 48),
        name="short_conv",
    )(proj, w, bias.reshape(1, width))


def dft_matrices(length):
    n = 2 * length
    kb = min(512, length)
    nkb = length // kb
    k = jnp.arange(length, dtype=jnp.int32)[:, None]
    s = jnp.arange(length, dtype=jnp.int32)[None, :]
    ang = ((k * s) % n).astype(F32) * (2.0 * math.pi / n)
    c, sn = jnp.cos(ang), jnp.sin(ang)
    alt_s = jnp.where(s % 2 == 0, 1.0, -1.0).astype(F32)
    alt_t = jnp.where(k % 2 == 0, 1.0, -1.0).astype(F32)
    f_re = c.astype(BF16)
    f_im = jnp.where(k == 0, alt_s, -sn).astype(BF16)
    g_c = jnp.where(s == 0, 1.0 / n, c * (2.0 / n)).astype(BF16)
    g_s = jnp.where(s == 0, alt_t / n, -sn * (2.0 / n)).astype(BF16)
    blk = lambda i: slice(i * kb, (i + 1) * kb)
    f_fwd = jnp.concatenate([m[blk(i)] for i in range(nkb) for m in (f_re, f_im)], axis=0)
    g_inv = jnp.concatenate([m[:, blk(i)] for i in range(nkb) for m in (g_c, g_s)], axis=1)
    return f_fwd, g_inv


def hyena_filter_spectrum(length, w1, b1, w2, b2, w3, freq, decay, f_fwd):
    hp = lax.Precision.HIGHEST
    pos = jnp.arange(length, dtype=F32)
    tt = (pos / max(length - 1, 1))[:, None]
    bands = jnp.linspace(1e-4, HYENA_BANDS - 1, HYENA_BANDS, dtype=F32)
    ang = (2.0 * math.pi / length) * pos[:, None] * bands[None, :]
    feats = jnp.concatenate([tt, jnp.cos(ang), jnp.sin(ang)], axis=-1)
    h = jnp.sin(freq[0] * (jnp.dot(feats, w1, precision=hp) + b1))
    h = jnp.sin(freq[1] * (jnp.dot(h, w2, precision=hp) + b2))
    h = jnp.dot(h, w3, precision=hp).reshape(length, 2, 2, HYENA_WIDTH)
    h = h * jnp.exp(-tt[:, :, None, None] * jnp.abs(decay))
    width = 2 * HYENA_WIDTH
    fwd = h[:, :, 0].reshape(length, width)
    bwd = jnp.where(pos[:, None] == 0, 0.0, h[:, :, 1].reshape(length, width))
    inv_l1 = 1.0 / (jnp.sum(jnp.abs(fwd), axis=0) + jnp.sum(jnp.abs(bwd), axis=0))
    n = 2 * length
    kb = min(512, length)
    spec = matmul(f_fwd, jnp.concatenate([fwd, bwd], axis=1).astype(BF16), 2 * width, F32,
                  tm=min(1024, n), tn=512, name="filter_dft")
    re_rows = lambda i: slice(2 * i * kb, (2 * i + 1) * kb)
    im_rows = lambda i: slice((2 * i + 1) * kb, (2 * i + 2) * kb)
    nkb = length // kb
    k_re = jnp.concatenate([spec[re_rows(i), :width] + spec[re_rows(i), width:] for i in range(nkb)], axis=0) * inv_l1
    k_im = jnp.concatenate([spec[im_rows(i), :width] - spec[im_rows(i), width:] for i in range(nkb)], axis=0) * inv_l1
    nyquist = (spec[kb, :width] + spec[kb, width:]) * inv_l1
    first = jnp.arange(length)[:, None] == 0
    return k_re, jnp.where(first, 0.0, k_im), jnp.where(first, nyquist[None, :], k_re)


def _dft_fwd_kernel(f_ref, z_ref, a_ref, b_ref, a2_ref, o_ref):
    res = jnp.dot(f_ref[...], z_ref[...], preferred_element_type=F32)
    kb = res.shape[0] // 2
    z_re, z_im = res[:kb], res[kb:]
    b = b_ref[...]
    o_ref[:kb, :] = (z_re * a_ref[...] - z_im * b).astype(o_ref.dtype)
    o_ref[kb:, :] = (z_re * b + z_im * a2_ref[...]).astype(o_ref.dtype)


def dft_forward(f_fwd, z, z_col, coefs, order, batch, length):
    n = 2 * length
    kb = min(512, length)
    nkb = length // kb
    nc = HYENA_WIDTH // HY_TC
    coef = pl.BlockSpec((kb, HY_TC), lambda b, c, k: (k, order * nc + c))
    return pl.pallas_call(
        _dft_fwd_kernel,
        out_shape=jax.ShapeDtypeStruct((batch * n, HYENA_WIDTH), BF16),
        grid=(batch, nc, nkb),
        in_specs=[pl.BlockSpec((2 * kb, length), lambda b, c, k: (k, 0)),
                  pl.BlockSpec((length, HY_TC), lambda b, c, k: (b, z_col + c)),
                  coef, coef, coef],
        out_specs=pl.BlockSpec((2 * kb, HY_TC), lambda b, c, k: (b * nkb + k, c)),
        compiler_params=_params(("parallel", "parallel", "parallel"), 40),
        name="dft_forward",
    )(f_fwd, z, *coefs)


def _dft_inv_kernel(g_ref, y_ref, x_ref, z_ref, bias_ref, o_ref):
    y = jnp.dot(g_ref[...], y_ref[...], preferred_element_type=F32)
    o_ref[...] = (x_ref[...].astype(F32) * (y + bias_ref[...] * z_ref[...].astype(F32))).astype(o_ref.dtype)


def dft_inverse(g_inv, spec, gate, gate_col, z, z_col, bias, batch, length):
    n = 2 * length
    tt = min(512, length)
    nt = length // tt
    nc = HYENA_WIDTH // HY_TC
    return pl.pallas_call(
        _dft_inv_kernel,
        out_shape=jax.ShapeDtypeStruct((batch * length, HYENA_WIDTH), BF16),
        grid=(batch, nc, nt),
        in_specs=[pl.BlockSpec((tt, n), lambda b, c, i: (i, 0)),
                  pl.BlockSpec((n, HY_TC), lambda b, c, i: (b, c)),
                  pl.BlockSpec((tt, HY_TC), lambda b, c, i: (b * nt + i, gate_col + c)),
                  pl.BlockSpec((tt, HY_TC), lambda b, c, i: (b * nt + i, z_col + c)),
                  pl.BlockSpec((1, HY_TC), lambda b, c, i: (0, c))],
        out_specs=pl.BlockSpec((tt, HY_TC), lambda b, c, i: (b * nt + i, c)),
        compiler_params=_params(("parallel", "parallel", "parallel"), 40),
        name="dft_inverse",
    )(g_inv, spec, gate, z, bias.reshape(1, HYENA_WIDTH))


def hyena_mixer(proj, conv_w, conv_b, coefs, bias, f_fwd, g_inv, batch, length):
    nc = HYENA_WIDTH // HY_TC
    cv = short_conv(proj, conv_w, conv_b, batch, length)
    spec = dft_forward(f_fwd, cv, 0, coefs, 0, batch, length)
    z1 = dft_inverse(g_inv, spec, cv, nc, cv, 0, bias[0], batch, length)
    spec = dft_forward(f_fwd, z1, 0, coefs, 1, batch, length)
    return dft_inverse(g_inv, spec, cv, 2 * nc, z1, 0, bias[1], batch, length)


def _merge_kernel(h_ref, g0, g1, g2, g3, y0, y1, y2, y3, wb_ref, o_ref):
    h = h_ref[...]
    acc = None
    for i, (g_ref, y_ref) in enumerate(((g0, y0), (g1, y1), (g2, y2), (g3, y3))):
        gate = jax.nn.sigmoid(jnp.dot(h, g_ref[...], preferred_element_type=F32))
        term = gate * jnp.dot(y_ref[...], wb_ref[i], preferred_element_type=F32)
        acc = term if acc is None else acc + term
    o_ref[...] = acc.astype(o_ref.dtype)


def gated_merge(h, w_packed, ys, w_branch, tm=512, tn=256):
    t, d = h.shape
    gate_spec = lambda i: pl.BlockSpec((d, tn), lambda n, m: (0, (OFF_GATE + i * d) // tn + n))
    y_spec = pl.BlockSpec((tm, BRANCH_WIDTH), lambda n, m: (m, 0))
    return pl.pallas_call(
        _merge_kernel,
        out_shape=jax.ShapeDtypeStruct((t, d), BF16),
        grid=(d // tn, t // tm),
        in_specs=[pl.BlockSpec((tm, d), lambda n, m: (m, 0))] + [gate_spec(i) for i in range(4)] + [y_spec] * 4
                 + [pl.BlockSpec((4, BRANCH_WIDTH, tn), lambda n, m: (0, 0, n))],
        out_specs=pl.BlockSpec((tm, tn), lambda n, m: (m, n)),
        compiler_params=_params(("parallel", "parallel"), 52),
        name="gated_merge",
    )(h, w_packed, w_packed, w_packed, w_packed, *ys, w_branch)


def moe_routing(logits, rg_b, re_b, tm, n_tiles):
    t = logits.shape[0]
    g_prob = jax.nn.softmax(logits[:, :N_GROUPS] + rg_b, axis=-1)
    g_idx = jnp.argmax(g_prob, axis=-1)
    g_val = jnp.max(g_prob, axis=-1)
    e_logit = (logits[:, N_GROUPS:N_GROUPS + N_EXPERTS] + re_b).reshape(t, N_GROUPS, EXPERTS_PER_GROUP)
    e_sel = jnp.take_along_axis(e_logit, g_idx[:, None, None], axis=1)[:, 0]
    top_val, top_idx = lax.top_k(jax.nn.softmax(e_sel, axis=-1), 2)
    top_val = top_val / jnp.sum(top_val, axis=-1, keepdims=True)
    weight = (g_val[:, None] * top_val).T.reshape(-1)
    expert = (g_idx[:, None] * EXPERTS_PER_GROUP + top_idx).T.reshape(-1).astype(jnp.int32)

    order = jnp.argsort(expert, stable=True).astype(jnp.int32)
    counts = jnp.sum(expert[None, :] == jnp.arange(N_EXPERTS, dtype=jnp.int32)[:, None], axis=1, dtype=jnp.int32)
    padded = ((counts + tm - 1) // tm) * tm
    pad_end = jnp.cumsum(padded)
    pad_start = pad_end - padded
    start = jnp.cumsum(counts) - counts
    tile_start = jnp.arange(n_tiles, dtype=jnp.int32) * tm
    tile_expert = jnp.minimum(jnp.sum(pad_end[None, :] <= tile_start[:, None], axis=1, dtype=jnp.int32), N_EXPERTS - 1)
    n_used = (pad_end[-1] // tm).astype(jnp.int32).reshape(1)
    in_tile = jnp.arange(tm, dtype=jnp.int32)[None, :]
    off = (tile_start - pad_start[tile_expert])[:, None] + in_tile
    valid = (off < counts[tile_expert][:, None]) & (tile_start < pad_end[-1])[:, None]
    src = order[jnp.clip(start[tile_expert][:, None] + off, 0, 2 * t - 1)]
    row_tok = jnp.where(valid, src % t, 0).astype(jnp.int32)
    row_dst = jnp.where(valid, src, 2 * t + in_tile).astype(jnp.int32)
    row_w = jnp.where(valid, weight[src], 0.0).astype(F32)
    return tile_expert, n_used, row_tok, row_dst, row_w


MOE_UNROLL = 32


def _moe_kernel(te_ref, nu_ref, tok_ref, tok_next_ref, dst_ref, w_ref, x_hbm, g_ref, wg_ref, wu_ref, wd_ref, o_hbm,
                xbuf, obuf, gsem, ssem):
    tm = xbuf.shape[1]
    i = pl.program_id(0)
    n_used = nu_ref[0]
    slot = i % 2

    def issue_gather(idx_ref, s):
        @pl.loop(0, tm // MOE_UNROLL)
        def _(c):
            for j in range(MOE_UNROLL):
                r = c * MOE_UNROLL + j
                pltpu.make_async_copy(x_hbm.at[pl.ds(idx_ref[0, 0, r], 1)], xbuf.at[s, pl.ds(r, 1)],
                                      gsem.at[s]).start()

    def wait_gather(s):
        pltpu.make_async_copy(x_hbm.at[pl.ds(0, tm)], xbuf.at[s], gsem.at[s]).wait()

    def wait_scatter(s):
        pltpu.make_async_copy(obuf.at[s], o_hbm.at[pl.ds(0, tm)], ssem.at[s]).wait()

    @pl.when(i < n_used)
    def _():
        @pl.when(i == 0)
        def _():
            issue_gather(tok_ref, slot)
            obuf[1] = jnp.zeros(obuf.shape[1:], obuf.dtype)
            init = pltpu.make_async_copy(obuf.at[1], o_hbm.at[pl.ds(o_hbm.shape[0] - tm, tm)], ssem.at[1])
            init.start()
            init.wait()

        wait_gather(slot)
        xn = _norm(xbuf[slot], g_ref[...]).astype(BF16)
        issue_gather(tok_next_ref, 1 - slot)
        a = jnp.dot(xn, wg_ref[0], preferred_element_type=F32)
        u = jnp.dot(xn, wu_ref[0], preferred_element_type=F32)
        hid = (a * jax.nn.sigmoid(a) * u * w_ref[...]).astype(BF16)
        res = jnp.dot(hid, wd_ref[0], preferred_element_type=F32)

        @pl.when(i >= 2)
        def _():
            wait_scatter(slot)

        obuf[slot] = res

        @pl.loop(0, tm // MOE_UNROLL)
        def _(c):
            for j in range(MOE_UNROLL):
                r = c * MOE_UNROLL + j
                pltpu.make_async_copy(obuf.at[slot, pl.ds(r, 1)], o_hbm.at[pl.ds(dst_ref[0, 0, r], 1)],
                                      ssem.at[slot]).start()

        @pl.when(i == n_used - 1)
        def _():
            wait_gather(1 - slot)
            wait_scatter(slot)

            @pl.when(i >= 1)
            def _():
                wait_scatter(1 - slot)


def moe_experts(x, g, schedule, w_gate, w_up, w_down, layer, tm):
    t, d = x.shape
    tile_expert, n_used, row_tok, row_dst, row_w = schedule
    n_tiles = tile_expert.shape[0]
    w_idx = lambda i, te, nu: (layer * N_EXPERTS + te[i], 0, 0)
    idx_spec = lambda off: pl.BlockSpec((1, 1, tm), lambda i, te, nu: (jnp.minimum(i + off, n_tiles - 1), 0, 0),
                                        memory_space=pltpu.SMEM)
    grid_spec = pltpu.PrefetchScalarGridSpec(
        num_scalar_prefetch=2,
        grid=(n_tiles,),
        in_specs=[idx_spec(0), idx_spec(1), idx_spec(0),
                  pl.BlockSpec((tm, 1), lambda i, te, nu: (i, 0)),
                  pl.BlockSpec(memory_space=pl.ANY),
                  pl.BlockSpec((1, d), lambda i, te, nu: (0, 0)),
                  pl.BlockSpec((1, d, EXPERT_FF), w_idx),
                  pl.BlockSpec((1, d, EXPERT_FF), w_idx),
                  pl.BlockSpec((1, EXPERT_FF, d), w_idx)],
        out_specs=pl.BlockSpec(memory_space=pl.ANY),
        scratch_shapes=[pltpu.VMEM((2, tm, d), F32), pltpu.VMEM((2, tm, d), F32),
                        pltpu.SemaphoreType.DMA((2,)), pltpu.SemaphoreType.DMA((2,))],
    )
    tok3 = row_tok.reshape(n_tiles, 1, tm)
    return pl.pallas_call(
        _moe_kernel,
        out_shape=jax.ShapeDtypeStruct((2 * t + tm, d), F32),
        grid_spec=grid_spec,
        compiler_params=_params(("arbitrary",), 56),
        name="moe_experts",
    )(tile_expert, n_used, tok3, tok3, row_dst.reshape(n_tiles, 1, tm),
      row_w.reshape(n_tiles * tm, 1), x, g.reshape(1, d), w_gate, w_up, w_down)


def pack_w_in(w):
    d = w.shape[0]
    s = [0, 768, 1280, 1344, 4416, 5440, 6464, 7488, 8512, 9536, 10560]
    c_q, c_kv, k_rope, u_hy = w[:, s[0]:s[1]], w[:, s[1]:s[2]], w[:, s[2]:s[3]], w[:, s[3]:s[4]]
    z = lambda n: jnp.zeros((d, n), w.dtype)
    return jnp.concatenate([w[:, s[4]:s[10]], u_hy, c_q, k_rope, z(64), z(128), c_kv, w[:, s[10]:]],
                           axis=1).astype(BF16)


def pack_w_uq(w):
    r = w.shape[0]
    w = w.reshape(r, N_HEADS, HEAD_DIM + MLA_ROPE_DIM)
    w = jnp.concatenate([w, jnp.zeros((r, N_HEADS, HEAD_DIM - MLA_ROPE_DIM), w.dtype)], axis=-1)
    return w.reshape(r, N_HEADS * 2 * HEAD_DIM).astype(BF16)


def pack_router(rg_w, re_w):
    d = rg_w.shape[0]
    pad = jnp.zeros((d, HEAD_DIM - N_GROUPS - N_EXPERTS), rg_w.dtype)
    return jnp.concatenate([rg_w, re_w, pad], axis=1).astype(BF16)


def kernel(x, norm_mix_g, w_in, mla_q_norm_g, mla_kv_norm_g, mla_w_uq, mla_w_ukv, hyena_conv_w, hyena_conv_b, hyena_ffn_w1, hyena_ffn_b1, hyena_ffn_w2, hyena_ffn_b2, hyena_ffn_w3, hyena_sin_freq, hyena_decay, hyena_bias, diff_lambda, diff_subln_g, na_rpb, w_branch, w_out, norm_ffn_g, router_group_w, router_group_b, router_expert_w, router_expert_b, moe_w_gate, moe_w_up, moe_w_down, norm_final_g):
    batch, length, d = x.shape
    t = batch * length
    assert d == D_MODEL and length % GRID_W == 0 and t % 1024 == 0
    tq = min(512, length)
    n_tiles = 2 * t // MOE_TM + N_EXPERTS

    rope_mla = rope_tables(length, 1)
    rope_diff = rope_tables(length, 2)
    f_fwd, g_inv = dft_matrices(length)
    moe_wg = moe_w_gate.astype(BF16).reshape(DEPTH * N_EXPERTS, d, EXPERT_FF)
    moe_wu = moe_w_up.astype(BF16).reshape(DEPTH * N_EXPERTS, d, EXPERT_FF)
    moe_wd = moe_w_down.astype(BF16).reshape(DEPTH * N_EXPERTS, EXPERT_FF, d)

    x2 = x.reshape(t, d)
    h = rms_norm(x2, norm_mix_g[0], BF16)
    out = None
    for l in range(DEPTH):
        w_packed = pack_w_in(w_in[l])
        proj = matmul(h, w_packed, N_PROJ, BF16, tm=1024, tn=512, name="in_proj")

        qf = norm_matmul(proj, OFF_CQ // MLA_Q_RANK, MLA_Q_RANK, mla_q_norm_g[l], pack_w_uq(mla_w_uq[l]), BF16,
                         tm=512, name="mla_q_up")
        kv = norm_matmul(proj, OFF_CKV // MLA_KV_RANK, MLA_KV_RANK, mla_kv_norm_g[l], mla_w_ukv[l].astype(BF16), BF16,
                         tm=512, name="mla_kv_up")
        y_a = mla_attention(qf, kv, proj, rope_mla, batch, length, tq)

        coefs = hyena_filter_spectrum(length, hyena_ffn_w1[l], hyena_ffn_b1[l], hyena_ffn_w2[l], hyena_ffn_b2[l],
                                      hyena_ffn_w3[l], hyena_sin_freq[l], hyena_decay[l], f_fwd)
        y_b = hyena_mixer(proj, hyena_conv_w[l], hyena_conv_b[l], coefs, hyena_bias[l], f_fwd, g_inv, batch, length)

        lambda_init = 0.8 - 0.6 * math.exp(-0.3 * l)
        lam = diff_lambda[l].astype(F32)
        lam_full = jnp.exp(jnp.sum(lam[0] * lam[1])) - jnp.exp(jnp.sum(lam[2] * lam[3])) + lambda_init
        y_c = diff_attention(proj, rope_diff, lam_full, diff_subln_g[l], lambda_init, batch, length, tq)

        y_d = na_attention(proj, na_bias_slabs(na_rpb[l]), batch, length)

        merged = gated_merge(h, w_packed, (y_a, y_b, y_c, y_d), w_branch[l].astype(BF16))
        x_mid = matmul(merged, w_out[l].astype(BF16), d, F32, tm=1024, tn=512, res=x2, name="out_proj")

        logits = norm_matmul(x_mid, 0, d, norm_ffn_g[l], pack_router(router_group_w[l], router_expert_w[l]), F32,
                             tm=256, name="router")
        schedule = moe_routing(logits, router_group_b[l], router_expert_b[l], MOE_TM, n_tiles)
        out2 = moe_experts(x_mid, norm_ffn_g[l], schedule, moe_wg, moe_wu, moe_wd, l, MOE_TM)
        if l + 1 < DEPTH:
            x2, h = combine_rms_norm(x_mid, out2, norm_mix_g[l + 1], BF16, emit_x=True)
        else:
            out = combine_rms_norm(x_mid, out2, norm_final_g, F32, emit_x=False)[0]
    return out.reshape(batch, length, d)
```

```python
import functools
import math

import jax
import jax.numpy as jnp
from jax import lax
from jax.experimental import pallas as pl
from jax.experimental.pallas import tpu as pltpu

F32 = jnp.float32
BF16 = jnp.bfloat16

D_MODEL = 4096
DEPTH = 2
NORM_EPS = 1e-6
ROPE_THETA = 10000.0
BRANCH_WIDTH = D_MODEL // 4
HEAD_DIM = 128
N_HEADS = BRANCH_WIDTH // HEAD_DIM
MLA_ROPE_DIM = 64
MLA_Q_RANK = (3 * D_MODEL) // 16
MLA_KV_RANK = 512
HYENA_WIDTH = BRANCH_WIDTH
HYENA_BANDS = 16
HYENA_DECAY_TARGET = 1e-2
DIFF_QK_DIM = 64
GRID_W = 64
NA_ROWS_MAX = 8
NA_COLS = 16
N_GROUPS = 4
EXPERTS_PER_GROUP = 8
N_EXPERTS = N_GROUPS * EXPERTS_PER_GROUP
EXPERT_FF = D_MODEL // 8
MASK_VALUE = -1e30

OFF_QD, OFF_KD, OFF_VD = 0, 1024, 2048
OFF_QNA, OFF_KNA, OFF_VNA = 3072, 4096, 5120
OFF_HY = 6144
OFF_CQ = 9216
OFF_KR = 9984
OFF_CKV = 10240
N_PROJ = 10752
OFF_GATE = N_PROJ
N_PACKED = N_PROJ + 4 * D_MODEL

MOE_TM = 256


def _params(semantics, vmem_mb):
    return pltpu.CompilerParams(dimension_semantics=semantics, vmem_limit_bytes=vmem_mb << 20)


def _norm(x, g):
    return x * lax.rsqrt(jnp.mean(x * x, axis=-1, keepdims=True) + NORM_EPS) * g


def _rms_kernel(x_ref, g_ref, h_ref):
    h_ref[...] = _norm(x_ref[...], g_ref[...]).astype(h_ref.dtype)


def rms_norm(x, g, out_dtype, tm=256):
    t, d = x.shape
    return pl.pallas_call(
        _rms_kernel,
        out_shape=jax.ShapeDtypeStruct((t, d), out_dtype),
        grid=(t // tm,),
        in_specs=[pl.BlockSpec((tm, d), lambda i: (i, 0)), pl.BlockSpec((1, d), lambda i: (0, 0))],
        out_specs=pl.BlockSpec((tm, d), lambda i: (i, 0)),
        compiler_params=_params(("parallel",), 40),
        name="rms_norm",
    )(x, g.reshape(1, d))


def _bf16_bits(a):
    return lax.bitcast_convert_type(a.astype(BF16).astype(F32), jnp.uint32)


def pack_halves(a):
    n = a.shape[1] // 2
    return (_bf16_bits(a[:, :n]) >> 16) | _bf16_bits(a[:, n:])


def unpack_halves(w):
    lo = lax.bitcast_convert_type(w << 16, F32)
    hi = lax.bitcast_convert_type(w & jnp.uint32(0xFFFF0000), F32)
    return jnp.concatenate([lo, hi], axis=1)


def _combine_rms_kernel(x_ref, a_ref, b_ref, g_ref, *out_refs):
    x = x_ref[...] + unpack_halves(a_ref[...]) + unpack_halves(b_ref[...])
    if len(out_refs) == 2:
        out_refs[0][...] = x
    out_refs[-1][...] = _norm(x, g_ref[...]).astype(out_refs[-1].dtype)


def combine_rms_norm(x, out2, g, out_dtype, emit_x, tm=256):
    t, d = x.shape
    nb = t // tm
    row = pl.BlockSpec((tm, d), lambda i: (i, 0))
    shapes = [jax.ShapeDtypeStruct((t, d), out_dtype)]
    if emit_x:
        shapes = [jax.ShapeDtypeStruct((t, d), F32)] + shapes
    return pl.pallas_call(
        _combine_rms_kernel,
        out_shape=shapes,
        grid=(nb,),
        in_specs=[row, pl.BlockSpec((tm, d // 2), lambda i: (i, 0)), pl.BlockSpec((tm, d // 2), lambda i: (i + nb, 0)),
                  pl.BlockSpec((1, d), lambda i: (0, 0))],
        out_specs=[row] * len(shapes),
        compiler_params=_params(("parallel",), 56),
        name="combine_rms_norm",
    )(x, out2, out2, g.reshape(1, d))


def _mm_kernel(a_ref, w_ref, o_ref):
    o_ref[...] = jnp.dot(a_ref[...], w_ref[...], preferred_element_type=F32).astype(o_ref.dtype)


def _mm_res_kernel(a_ref, w_ref, r_ref, o_ref):
    o_ref[...] = r_ref[...] + jnp.dot(a_ref[...], w_ref[...], preferred_element_type=F32)


def matmul(a, w, n_out, out_dtype, tm, tn, res=None, vmem_mb=48, name="matmul"):
    m, k = a.shape
    in_specs = [pl.BlockSpec((tm, k), lambda i, j: (i, 0)), pl.BlockSpec((k, tn), lambda i, j: (0, j))]
    args = [a, w]
    body = _mm_kernel
    if res is not None:
        in_specs.append(pl.BlockSpec((tm, tn), lambda i, j: (i, j)))
        args.append(res)
        body = _mm_res_kernel
    return pl.pallas_call(
        body,
        out_shape=jax.ShapeDtypeStruct((m, n_out), out_dtype),
        grid=(m // tm, n_out // tn),
        in_specs=in_specs,
        out_specs=pl.BlockSpec((tm, tn), lambda i, j: (i, j)),
        compiler_params=_params(("parallel", "parallel"), vmem_mb),
        name=name,
    )(*args)


def _norm_mm_kernel(a_ref, g_ref, w_ref, o_ref):
    an = _norm(a_ref[...].astype(F32), g_ref[...]).astype(BF16)
    o_ref[...] = jnp.dot(an, w_ref[...], preferred_element_type=F32).astype(o_ref.dtype)


def _router_kernel(x_ref, g_ref, w_ref, logits_ref, xp_ref):
    xn = _norm(x_ref[...], g_ref[...])
    logits_ref[...] = jnp.dot(xn.astype(BF16), w_ref[...], preferred_element_type=F32)
    xp_ref[...] = pack_halves(xn)


def router(x, g, w, tm=256):
    t, d = x.shape
    n = w.shape[1]
    return pl.pallas_call(
        _router_kernel,
        out_shape=[jax.ShapeDtypeStruct((t, n), F32), jax.ShapeDtypeStruct((t, d // 2), jnp.uint32)],
        grid=(t // tm,),
        in_specs=[pl.BlockSpec((tm, d), lambda i: (i, 0)),
                  pl.BlockSpec((1, d), lambda i: (0, 0)),
                  pl.BlockSpec((d, n), lambda i: (0, 0))],
        out_specs=[pl.BlockSpec((tm, n), lambda i: (i, 0)), pl.BlockSpec((tm, d // 2), lambda i: (i, 0))],
        compiler_params=_params(("parallel",), 40),
        name="router",
    )(x, g.reshape(1, d), w)


def norm_matmul(a, col_block, k, g, w, out_dtype, tm, vmem_mb=40, name="norm_matmul"):
    m = a.shape[0]
    n = w.shape[1]
    return pl.pallas_call(
        _norm_mm_kernel,
        out_shape=jax.ShapeDtypeStruct((m, n), out_dtype),
        grid=(m // tm,),
        in_specs=[pl.BlockSpec((tm, k), lambda i: (i, col_block)),
                  pl.BlockSpec((1, k), lambda i: (0, 0)),
                  pl.BlockSpec((k, n), lambda i: (0, 0))],
        out_specs=pl.BlockSpec((tm, n), lambda i: (i, 0)),
        compiler_params=_params(("parallel",), vmem_mb),
        name=name,
    )(a, g.reshape(1, k), w)


def _rope(x, cos_t, sin_a, sin_b):
    return x * cos_t + pltpu.roll(x, 96, 1) * sin_a + pltpu.roll(x, 32, 1) * sin_b


def rope_tables(length, chunks):
    half = MLA_ROPE_DIM // 2
    inv = 1.0 / (ROPE_THETA ** (jnp.arange(0, MLA_ROPE_DIM, 2, dtype=F32) / MLA_ROPE_DIM))
    ang = jnp.arange(length, dtype=F32)[:, None] * inv[None, :]
    cos, sin = jnp.cos(ang), jnp.sin(ang)
    zero = jnp.zeros((length, half), F32)
    c, sa, sb = [], [], []
    for i in range(2):
        on = i < chunks
        c += [cos, cos] if on else [zero, zero]
        sa += [-sin, zero] if on else [zero, zero]
        sb += [zero, sin] if on else [zero, zero]
    return jnp.concatenate(c, 1), jnp.concatenate(sa, 1), jnp.concatenate(sb, 1)


LOG2E = math.log2(math.e)


def _softmax_pv(s2, v):
    m = jnp.max(s2, axis=-1, keepdims=True)
    p = jnp.exp2(s2 - m)
    l = jnp.sum(p, axis=-1, keepdims=True)
    return jnp.dot(p.astype(BF16), v, preferred_element_type=F32) / l


def _qk(q, k):
    return lax.dot_general(q, k, (((1,), (1,)), ((), ())), preferred_element_type=F32)


def _mla_kernel(q_ref, kn_ref, v_ref, kr_ref, cq_ref, saq_ref, sbq_ref, ck_ref, sak_ref, sbk_ref, o_ref, *, scale):
    q = q_ref[...]
    c = scale * LOG2E
    qn = q[:, :HEAD_DIM].astype(F32) * c
    qr = _rope(q[:, HEAD_DIM:].astype(F32), cq_ref[...], saq_ref[...], sbq_ref[...]) * c
    qf = jnp.concatenate([qn.astype(BF16), qr.astype(BF16)], axis=1)
    kr = _rope(kr_ref[...].astype(F32), ck_ref[...], sak_ref[...], sbk_ref[...])
    kf = jnp.concatenate([kn_ref[...], kr.astype(BF16)], axis=1)
    v = v_ref[...]
    part = qf.shape[0] // 2
    for r in range(2):
        rows = slice(r * part, (r + 1) * part)
        o_ref[rows, :] = _softmax_pv(_qk(qf[rows], kf), v).astype(o_ref.dtype)


def mla_attention(qf, kv, proj, tabs, batch, length, tq):
    t = batch * length
    nq = length // tq
    scale = (HEAD_DIM + MLA_ROPE_DIM) ** -0.5
    tab_q = pl.BlockSpec((tq, HEAD_DIM), lambda b, h, i: (i, 0))
    tab_k = pl.BlockSpec((length, HEAD_DIM), lambda b, h, i: (0, 0))
    return pl.pallas_call(
        functools.partial(_mla_kernel, scale=scale),
        out_shape=jax.ShapeDtypeStruct((t, BRANCH_WIDTH), BF16),
        grid=(batch, N_HEADS, nq),
        in_specs=[pl.BlockSpec((tq, 2 * HEAD_DIM), lambda b, h, i: (b * nq + i, h)),
                  pl.BlockSpec((length, HEAD_DIM), lambda b, h, i: (b, 2 * h)),
                  pl.BlockSpec((length, HEAD_DIM), lambda b, h, i: (b, 2 * h + 1)),
                  pl.BlockSpec((length, HEAD_DIM), lambda b, h, i: (b, OFF_KR // HEAD_DIM)),
                  tab_q, tab_q, tab_q, tab_k, tab_k, tab_k],
        out_specs=pl.BlockSpec((tq, HEAD_DIM), lambda b, h, i: (b * nq + i, h)),
        compiler_params=_params(("parallel", "parallel", "parallel"), 48),
        name="mla_attention",
    )(qf, kv, kv, proj, *tabs, *tabs)


def _diff_kernel(q_ref, k_ref, v_ref, cq_ref, saq_ref, sbq_ref, ck_ref, sak_ref, sbk_ref, lam_ref, g_ref, o_ref,
                 *, scale, out_scale):
    q = _rope(q_ref[...].astype(F32), cq_ref[...], saq_ref[...], sbq_ref[...])
    q = q * (scale * LOG2E)
    k = _rope(k_ref[...].astype(F32), ck_ref[...], sak_ref[...], sbk_ref[...]).astype(BF16)
    lane = lax.broadcasted_iota(jnp.int32, q.shape, 1)
    q0 = jnp.where(lane < DIFF_QK_DIM, q, 0.0).astype(BF16)
    q1 = jnp.where(lane >= DIFF_QK_DIM, q, 0.0).astype(BF16)
    v = v_ref[...]
    half = q.shape[0] // 2
    for r in range(2):
        rows = slice(r * half, (r + 1) * half)
        o = _softmax_pv(_qk(q0[rows], k), v) - lam_ref[...] * _softmax_pv(_qk(q1[rows], k), v)
        o_ref[rows, :] = (_norm(o, g_ref[...]) * out_scale).astype(o_ref.dtype)


def diff_attention(proj, tabs, lam_full, subln_g, lambda_init, batch, length, tq):
    t = batch * length
    nq = length // tq
    tab_q = pl.BlockSpec((tq, HEAD_DIM), lambda b, h, i: (i, 0))
    tab_k = pl.BlockSpec((length, HEAD_DIM), lambda b, h, i: (0, 0))
    vec = pl.BlockSpec((1, HEAD_DIM), lambda b, h, i: (0, 0))
    return pl.pallas_call(
        functools.partial(_diff_kernel, scale=DIFF_QK_DIM ** -0.5, out_scale=1.0 - lambda_init),
        out_shape=jax.ShapeDtypeStruct((t, BRANCH_WIDTH), BF16),
        grid=(batch, N_HEADS, nq),
        in_specs=[pl.BlockSpec((tq, HEAD_DIM), lambda b, h, i: (b * nq + i, OFF_QD // HEAD_DIM + h)),
                  pl.BlockSpec((length, HEAD_DIM), lambda b, h, i: (b, OFF_KD // HEAD_DIM + h)),
                  pl.BlockSpec((length, HEAD_DIM), lambda b, h, i: (b, OFF_VD // HEAD_DIM + h)),
                  tab_q, tab_q, tab_q, tab_k, tab_k, tab_k, vec, vec],
        out_specs=pl.BlockSpec((tq, HEAD_DIM), lambda b, h, i: (b * nq + i, h)),
        compiler_params=_params(("parallel", "parallel", "parallel"), 48),
        name="diff_attention",
    )(proj, proj, proj, *tabs, *tabs, jnp.full((1, HEAD_DIM), lam_full, F32), subln_g.reshape(1, HEAD_DIM))


NA_GROUP = 4


def _na_kernel(q_ref, k_ref, v_ref, cb_ref, o_ref, *, rows, kr, win, scale):
    g = pl.program_id(1)
    w_start = jnp.clip(g * NA_GROUP - kr // 2, 0, rows - win)
    base = pl.multiple_of(w_start * GRID_W, GRID_W)
    nk = win * GRID_W
    left = lax.broadcasted_iota(jnp.int32, (GRID_W, 2 * GRID_W), 1) < GRID_W

    slab, row_mask = [], []
    for j in range(NA_GROUP):
        r = g * NA_GROUP + j
        r_start = jnp.clip(r - kr // 2, 0, rows - kr)
        slab_j, mask_j = [], []
        for p in range(win // 2):
            a = w_start + 2 * p
            ok_l = (a >= r_start) & (a < r_start + kr)
            ok_r = (a + 1 >= r_start) & (a + 1 < r_start + kr)
            slab_j.append(jnp.clip(a - r + NA_ROWS_MAX, 0, 2 * NA_ROWS_MAX - 1))
            mask_j.append(jnp.where(left, jnp.where(ok_l, 0.0, MASK_VALUE), jnp.where(ok_r, 0.0, MASK_VALUE)))
        slab.append(slab_j)
        row_mask.append(jnp.concatenate(mask_j, axis=1))

    outs = []
    for h in range(N_HEADS):
        cols = slice(h * HEAD_DIM, (h + 1) * HEAD_DIM)
        qh = (q_ref[:, cols].astype(F32) * (scale * LOG2E)).astype(BF16)
        bias = jnp.concatenate(
            [jnp.concatenate([cb_ref[h, slab[j][p]] for p in range(win // 2)], axis=1) + row_mask[j]
             for j in range(NA_GROUP)], axis=0)
        s2 = _qk(qh, k_ref[pl.ds(base, nk), cols]) + bias
        outs.append(_softmax_pv(s2, v_ref[pl.ds(base, nk), cols]))
    o_ref[...] = jnp.concatenate(outs, axis=1).astype(o_ref.dtype)


def na_bias_slabs(rpb):
    cols = jnp.arange(GRID_W)
    col_start = jnp.clip(cols - NA_COLS // 2, 0, GRID_W - NA_COLS)
    col_in = (cols[None, :] >= col_start[:, None]) & (cols[None, :] < col_start[:, None] + NA_COLS)
    col_idx = jnp.clip(cols[None, :] - cols[:, None], -(NA_COLS - 1), NA_COLS - 1) + NA_COLS - 1
    col_bias = jnp.where(col_in[None, None], rpb[:, :, col_idx].astype(F32) * LOG2E, MASK_VALUE)
    pad = jnp.full((N_HEADS, 1, GRID_W, GRID_W), MASK_VALUE, F32)
    ext = jnp.concatenate([pad, col_bias, pad], axis=1)
    return jnp.concatenate([ext[:, :-1], ext[:, 1:]], axis=-1)


def na_attention(proj, slabs, batch, length):
    t = batch * length
    rows = length // GRID_W
    kr = min(NA_ROWS_MAX, rows)
    win = min(rows, kr + NA_GROUP)
    assert win % 2 == 0 and rows % NA_GROUP == 0
    steps = rows // NA_GROUP
    kv_spec = lambda off: pl.BlockSpec((length, BRANCH_WIDTH), lambda b, r: (b, off // BRANCH_WIDTH))
    return pl.pallas_call(
        functools.partial(_na_kernel, rows=rows, kr=kr, win=win, scale=HEAD_DIM ** -0.5),
        out_shape=jax.ShapeDtypeStruct((t, BRANCH_WIDTH), BF16),
        grid=(batch, steps),
        in_specs=[pl.BlockSpec((NA_GROUP * GRID_W, BRANCH_WIDTH),
                               lambda b, r: (b * steps + r, OFF_QNA // BRANCH_WIDTH)),
                  kv_spec(OFF_KNA), kv_spec(OFF_VNA),
                  pl.BlockSpec((N_HEADS, 2 * NA_ROWS_MAX, GRID_W, 2 * GRID_W), lambda b, r: (0, 0, 0, 0))],
        out_specs=pl.BlockSpec((NA_GROUP * GRID_W, BRANCH_WIDTH), lambda b, r: (b * steps + r, 0)),
        compiler_params=_params(("parallel", "parallel"), 48),
        name="na_attention",
    )(proj, proj, proj, slabs)


HY_TC = 1024
CONV_TC = 512


def _short_conv_kernel(u_ref, w_ref, b_ref, o_ref):
    u = u_ref[...].astype(F32)
    n = u.shape[0]
    row = lax.broadcasted_iota(jnp.int32, u.shape, 0)
    prev = jnp.where(row == 0, 0.0, pltpu.roll(u, 1, 0))
    nxt = jnp.where(row == n - 1, 0.0, pltpu.roll(u, n - 1, 0))
    w = w_ref[...]
    o_ref[...] = (prev * w[0:1] + u * w[1:2] + nxt * w[2:3] + b_ref[...]).astype(o_ref.dtype)


def short_conv(proj, w, bias, batch, length):
    t = batch * length
    width = 3 * HYENA_WIDTH
    return pl.pallas_call(
        _short_conv_kernel,
        out_shape=jax.ShapeDtypeStruct((t, width), BF16),
        grid=(batch, width // CONV_TC),
        in_specs=[pl.BlockSpec((length, CONV_TC), lambda b, c: (b, OFF_HY // CONV_TC + c)),
                  pl.BlockSpec((3, CONV_TC), lambda b, c: (0, c)),
                  pl.BlockSpec((1, CONV_TC), lambda b, c: (0, c))],
        out_specs=pl.BlockSpec((length, CONV_TC), lambda b, c: (b, c)),
        compiler_params=_params(("parallel", "parallel"), 48),
        name="short_conv",
    )(proj, w, bias.reshape(1, width))


def dft_matrices(length):
    n = 2 * length
    kb = min(512, length)
    nkb = length // kb
    k = jnp.arange(length, dtype=jnp.int32)[:, None]
    s = jnp.arange(length, dtype=jnp.int32)[None, :]
    ang = ((k * s) % n).astype(F32) * (2.0 * math.pi / n)
    c, sn = jnp.cos(ang), jnp.sin(ang)
    alt_s = jnp.where(s % 2 == 0, 1.0, -1.0).astype(F32)
    alt_t = jnp.where(k % 2 == 0, 1.0, -1.0).astype(F32)
    f_re = c.astype(BF16)
    f_im = jnp.where(k == 0, alt_s, -sn).astype(BF16)
    g_c = jnp.where(s == 0, 1.0 / n, c * (2.0 / n)).astype(BF16)
    g_s = jnp.where(s == 0, alt_t / n, -sn * (2.0 / n)).astype(BF16)
    blk = lambda i: slice(i * kb, (i + 1) * kb)
    f_fwd = jnp.concatenate([m[blk(i)] for i in range(nkb) for m in (f_re, f_im)], axis=0)
    g_inv = jnp.concatenate([m[:, blk(i)] for i in range(nkb) for m in (g_c, g_s)], axis=1)
    return f_fwd, g_inv


def hyena_filter_spectrum(length, w1, b1, w2, b2, w3, freq, decay, f_fwd):
    hp = lax.Precision.HIGHEST
    pos = jnp.arange(length, dtype=F32)
    tt = (pos / max(length - 1, 1))[:, None]
    bands = jnp.linspace(1e-4, HYENA_BANDS - 1, HYENA_BANDS, dtype=F32)
    ang = (2.0 * math.pi / length) * pos[:, None] * bands[None, :]
    feats = jnp.concatenate([tt, jnp.cos(ang), jnp.sin(ang)], axis=-1)
    h = jnp.sin(freq[0] * (jnp.dot(feats, w1, precision=hp) + b1))
    h = jnp.sin(freq[1] * (jnp.dot(h, w2, precision=hp) + b2))
    h = jnp.dot(h, w3, precision=hp).reshape(length, 2, 2, HYENA_WIDTH)
    h = h * jnp.exp(-tt[:, :, None, None] * jnp.abs(decay))
    width = 2 * HYENA_WIDTH
    fwd = h[:, :, 0].reshape(length, width)
    bwd = jnp.where(pos[:, None] == 0, 0.0, h[:, :, 1].reshape(length, width))
    inv_l1 = 1.0 / (jnp.sum(jnp.abs(fwd), axis=0) + jnp.sum(jnp.abs(bwd), axis=0))
    n = 2 * length
    kb = min(512, length)
    spec = matmul(f_fwd, jnp.concatenate([fwd, bwd], axis=1).astype(BF16), 2 * width, F32,
                  tm=min(1024, n), tn=512, name="filter_dft")
    re_rows = lambda i: slice(2 * i * kb, (2 * i + 1) * kb)
    im_rows = lambda i: slice((2 * i + 1) * kb, (2 * i + 2) * kb)
    nkb = length // kb
    k_re = jnp.concatenate([spec[re_rows(i), :width] + spec[re_rows(i), width:] for i in range(nkb)], axis=0) * inv_l1
    k_im = jnp.concatenate([spec[im_rows(i), :width] - spec[im_rows(i), width:] for i in range(nkb)], axis=0) * inv_l1
    nyquist = (spec[kb, :width] + spec[kb, width:]) * inv_l1
    first = jnp.arange(length)[:, None] == 0
    return k_re, jnp.where(first, 0.0, k_im), jnp.where(first, nyquist[None, :], k_re)


def _dft_fwd_kernel(f_ref, z_ref, a_ref, b_ref, a2_ref, o_ref):
    res = jnp.dot(f_ref[...], z_ref[...], preferred_element_type=F32)
    kb = res.shape[0] // 2
    z_re, z_im = res[:kb], res[kb:]
    b = b_ref[...]
    o_ref[:kb, :] = (z_re * a_ref[...] - z_im * b).astype(o_ref.dtype)
    o_ref[kb:, :] = (z_re * b + z_im * a2_ref[...]).astype(o_ref.dtype)


def dft_forward(f_fwd, z, z_col, coefs, order, batch, length):
    n = 2 * length
    kb = min(512, length)
    nkb = length // kb
    nc = HYENA_WIDTH // HY_TC
    coef = pl.BlockSpec((kb, HY_TC), lambda b, c, k: (k, order * nc + c))
    return pl.pallas_call(
        _dft_fwd_kernel,
        out_shape=jax.ShapeDtypeStruct((batch * n, HYENA_WIDTH), BF16),
        grid=(batch, nc, nkb),
        in_specs=[pl.BlockSpec((2 * kb, length), lambda b, c, k: (k, 0)),
                  pl.BlockSpec((length, HY_TC), lambda b, c, k: (b, z_col + c)),
                  coef, coef, coef],
        out_specs=pl.BlockSpec((2 * kb, HY_TC), lambda b, c, k: (b * nkb + k, c)),
        compiler_params=_params(("parallel", "parallel", "parallel"), 56),
        name="dft_forward",
    )(f_fwd, z, *coefs)


def _dft_inv_kernel(g_ref, y_ref, x_ref, z_ref, bias_ref, o_ref):
    y = jnp.dot(g_ref[...], y_ref[...], preferred_element_type=F32)
    o_ref[...] = (x_ref[...].astype(F32) * (y + bias_ref[...] * z_ref[...].astype(F32))).astype(o_ref.dtype)


def dft_inverse(g_inv, spec, gate, gate_col, z, z_col, bias, batch, length):
    n = 2 * length
    tt = min(512, length)
    nt = length // tt
    nc = HYENA_WIDTH // HY_TC
    return pl.pallas_call(
        _dft_inv_kernel,
        out_shape=jax.ShapeDtypeStruct((batch * length, HYENA_WIDTH), BF16),
        grid=(batch, nc, nt),
        in_specs=[pl.BlockSpec((tt, n), lambda b, c, i: (i, 0)),
                  pl.BlockSpec((n, HY_TC), lambda b, c, i: (b, c)),
                  pl.BlockSpec((tt, HY_TC), lambda b, c, i: (b * nt + i, gate_col + c)),
                  pl.BlockSpec((tt, HY_TC), lambda b, c, i: (b * nt + i, z_col + c)),
                  pl.BlockSpec((1, HY_TC), lambda b, c, i: (0, c))],
        out_specs=pl.BlockSpec((tt, HY_TC), lambda b, c, i: (b * nt + i, c)),
        compiler_params=_params(("parallel", "parallel", "parallel"), 48),
        name="dft_inverse",
    )(g_inv, spec, gate, z, bias.reshape(1, HYENA_WIDTH))


def hyena_mixer(proj, conv_w, conv_b, coefs, bias, f_fwd, g_inv, batch, length):
    nc = HYENA_WIDTH // HY_TC
    cv = short_conv(proj, conv_w, conv_b, batch, length)
    spec = dft_forward(f_fwd, cv, 0, coefs, 0, batch, length)
    z1 = dft_inverse(g_inv, spec, cv, nc, cv, 0, bias[0], batch, length)
    spec = dft_forward(f_fwd, z1, 0, coefs, 1, batch, length)
    return dft_inverse(g_inv, spec, cv, 2 * nc, z1, 0, bias[1], batch, length)


def _merge_kernel(h_ref, g0, g1, g2, g3, y0, y1, y2, y3, wb_ref, o_ref):
    h = h_ref[...]
    acc = None
    for i, (g_ref, y_ref) in enumerate(((g0, y0), (g1, y1), (g2, y2), (g3, y3))):
        gate = jax.nn.sigmoid(jnp.dot(h, g_ref[...], preferred_element_type=F32))
        term = gate * jnp.dot(y_ref[...], wb_ref[i], preferred_element_type=F32)
        acc = term if acc is None else acc + term
    o_ref[...] = acc.astype(o_ref.dtype)


def gated_merge(h, w_packed, ys, w_branch, tm=512, tn=256):
    t, d = h.shape
    gate_spec = lambda i: pl.BlockSpec((d, tn), lambda n, m: (0, (OFF_GATE + i * d) // tn + n))
    y_spec = pl.BlockSpec((tm, BRANCH_WIDTH), lambda n, m: (m, 0))
    return pl.pallas_call(
        _merge_kernel,
        out_shape=jax.ShapeDtypeStruct((t, d), BF16),
        grid=(d // tn, t // tm),
        in_specs=[pl.BlockSpec((tm, d), lambda n, m: (m, 0))] + [gate_spec(i) for i in range(4)] + [y_spec] * 4
                 + [pl.BlockSpec((4, BRANCH_WIDTH, tn), lambda n, m: (0, 0, n))],
        out_specs=pl.BlockSpec((tm, tn), lambda n, m: (m, n)),
        compiler_params=_params(("parallel", "parallel"), 52),
        name="gated_merge",
    )(h, w_packed, w_packed, w_packed, w_packed, *ys, w_branch)


def moe_routing(logits, rg_b, re_b, tm, n_tiles):
    t = logits.shape[0]
    g_prob = jax.nn.softmax(logits[:, :N_GROUPS] + rg_b, axis=-1)
    g_idx = jnp.argmax(g_prob, axis=-1)
    g_val = jnp.max(g_prob, axis=-1)
    e_logit = (logits[:, N_GROUPS:N_GROUPS + N_EXPERTS] + re_b).reshape(t, N_GROUPS, EXPERTS_PER_GROUP)
    e_sel = jnp.take_along_axis(e_logit, g_idx[:, None, None], axis=1)[:, 0]
    top_val, top_idx = lax.top_k(jax.nn.softmax(e_sel, axis=-1), 2)
    top_val = top_val / jnp.sum(top_val, axis=-1, keepdims=True)
    weight = (g_val[:, None] * top_val).T.reshape(-1)
    expert = (g_idx[:, None] * EXPERTS_PER_GROUP + top_idx).T.reshape(-1).astype(jnp.int32)

    order = jnp.argsort(expert, stable=True).astype(jnp.int32)
    counts = jnp.sum(expert[None, :] == jnp.arange(N_EXPERTS, dtype=jnp.int32)[:, None], axis=1, dtype=jnp.int32)
    padded = ((counts + tm - 1) // tm) * tm
    pad_end = jnp.cumsum(padded)
    pad_start = pad_end - padded
    start = jnp.cumsum(counts) - counts
    tile_start = jnp.arange(n_tiles, dtype=jnp.int32) * tm
    tile_expert = jnp.minimum(jnp.sum(pad_end[None, :] <= tile_start[:, None], axis=1, dtype=jnp.int32), N_EXPERTS - 1)
    n_used = (pad_end[-1] // tm).astype(jnp.int32).reshape(1)
    in_tile = jnp.arange(tm, dtype=jnp.int32)[None, :]
    off = (tile_start - pad_start[tile_expert])[:, None] + in_tile
    valid = (off < counts[tile_expert][:, None]) & (tile_start < pad_end[-1])[:, None]
    src = order[jnp.clip(start[tile_expert][:, None] + off, 0, 2 * t - 1)]
    row_tok = jnp.where(valid, src % t, 0).astype(jnp.int32)
    row_dst = jnp.where(valid, src, 2 * t + in_tile).astype(jnp.int32)
    row_w = jnp.where(valid, weight[src], 0.0).astype(F32)
    return tile_expert, n_used, row_tok, row_dst, row_w


MOE_UNROLL = 32


def _moe_kernel(te_ref, nu_ref, tok_ref, tok_next_ref, dst_ref, w_ref, x_hbm, wg_ref, wu_ref, wd_ref, o_hbm,
                xbuf, obuf, gsem, ssem):
    tm = xbuf.shape[1]
    i = pl.program_id(0)
    n_used = nu_ref[0]
    slot = i % 2

    def issue_gather(idx_ref, s):
        @pl.loop(0, tm // MOE_UNROLL)
        def _(c):
            for j in range(MOE_UNROLL):
                r = c * MOE_UNROLL + j
                pltpu.make_async_copy(x_hbm.at[pl.ds(idx_ref[0, 0, r], 1)], xbuf.at[s, pl.ds(r, 1)],
                                      gsem.at[s]).start()

    def wait_gather(s):
        pltpu.make_async_copy(x_hbm.at[pl.ds(0, tm)], xbuf.at[s], gsem.at[s]).wait()

    def wait_scatter(s):
        pltpu.make_async_copy(obuf.at[s], o_hbm.at[pl.ds(0, tm)], ssem.at[s]).wait()

    @pl.when(i < n_used)
    def _():
        @pl.when(i == 0)
        def _():
            issue_gather(tok_ref, slot)
            obuf[1] = jnp.zeros(obuf.shape[1:], obuf.dtype)
            init = pltpu.make_async_copy(obuf.at[1], o_hbm.at[pl.ds(o_hbm.shape[0] - tm, tm)], ssem.at[1])
            init.start()
            init.wait()

        wait_gather(slot)
        xn = unpack_halves(xbuf[slot]).astype(BF16)
        issue_gather(tok_next_ref, 1 - slot)
        a = jnp.dot(xn, wg_ref[0], preferred_element_type=F32)
        u = jnp.dot(xn, wu_ref[0], preferred_element_type=F32)
        hid = (a * jax.nn.sigmoid(a) * u * w_ref[...]).astype(BF16)
        res = pack_halves(jnp.dot(hid, wd_ref[0], preferred_element_type=F32))

        @pl.when(i >= 2)
        def _():
            wait_scatter(slot)

        obuf[slot] = res

        @pl.loop(0, tm // MOE_UNROLL)
        def _(c):
            for j in range(MOE_UNROLL):
                r = c * MOE_UNROLL + j
                pltpu.make_async_copy(obuf.at[slot, pl.ds(r, 1)], o_hbm.at[pl.ds(dst_ref[0, 0, r], 1)],
                                      ssem.at[slot]).start()

        @pl.when(i == n_used - 1)
        def _():
            wait_gather(1 - slot)
            wait_scatter(slot)

            @pl.when(i >= 1)
            def _():
                wait_scatter(1 - slot)


def moe_experts(xp, schedule, w_gate, w_up, w_down, layer, tm):
    t, dp = xp.shape
    d = 2 * dp
    tile_expert, n_used, row_tok, row_dst, row_w = schedule
    n_tiles = tile_expert.shape[0]
    w_idx = lambda i, te, nu: (layer * N_EXPERTS + te[i], 0, 0)
    idx_spec = lambda off: pl.BlockSpec((1, 1, tm), lambda i, te, nu: (jnp.minimum(i + off, n_tiles - 1), 0, 0),
                                        memory_space=pltpu.SMEM)
    grid_spec = pltpu.PrefetchScalarGridSpec(
        num_scalar_prefetch=2,
        grid=(n_tiles,),
        in_specs=[idx_spec(0), idx_spec(1), idx_spec(0),
                  pl.BlockSpec((tm, 1), lambda i, te, nu: (i, 0)),
                  pl.BlockSpec(memory_space=pl.ANY),
                  pl.BlockSpec((1, d, EXPERT_FF), w_idx),
                  pl.BlockSpec((1, d, EXPERT_FF), w_idx),
                  pl.BlockSpec((1, EXPERT_FF, d), w_idx)],
        out_specs=pl.BlockSpec(memory_space=pl.ANY),
        scratch_shapes=[pltpu.VMEM((2, tm, dp), jnp.uint32), pltpu.VMEM((2, tm, dp), jnp.uint32),
                        pltpu.SemaphoreType.DMA((2,)), pltpu.SemaphoreType.DMA((2,))],
    )
    tok3 = row_tok.reshape(n_tiles, 1, tm)
    return pl.pallas_call(
        _moe_kernel,
        out_shape=jax.ShapeDtypeStruct((2 * t + tm, dp), jnp.uint32),
        grid_spec=grid_spec,
        compiler_params=_params(("arbitrary",), 48),
        name="moe_experts",
    )(tile_expert, n_used, tok3, tok3, row_dst.reshape(n_tiles, 1, tm),
      row_w.reshape(n_tiles * tm, 1), xp, w_gate, w_up, w_down)


def pack_w_in(w):
    d = w.shape[0]
    s = [0, 768, 1280, 1344, 4416, 5440, 6464, 7488, 8512, 9536, 10560]
    c_q, c_kv, k_rope, u_hy = w[:, s[0]:s[1]], w[:, s[1]:s[2]], w[:, s[2]:s[3]], w[:, s[3]:s[4]]
    z = lambda n: jnp.zeros((d, n), w.dtype)
    return jnp.concatenate([w[:, s[4]:s[10]], u_hy, c_q, k_rope, z(64), z(128), c_kv, w[:, s[10]:]],
                           axis=1).astype(BF16)


def pack_w_uq(w):
    r = w.shape[0]
    w = w.reshape(r, N_HEADS, HEAD_DIM + MLA_ROPE_DIM)
    w = jnp.concatenate([w, jnp.zeros((r, N_HEADS, HEAD_DIM - MLA_ROPE_DIM), w.dtype)], axis=-1)
    return w.reshape(r, N_HEADS * 2 * HEAD_DIM).astype(BF16)


def pack_router(rg_w, re_w):
    d = rg_w.shape[0]
    pad = jnp.zeros((d, HEAD_DIM - N_GROUPS - N_EXPERTS), rg_w.dtype)
    return jnp.concatenate([rg_w, re_w, pad], axis=1).astype(BF16)


def kernel(x, norm_mix_g, w_in, mla_q_norm_g, mla_kv_norm_g, mla_w_uq, mla_w_ukv, hyena_conv_w, hyena_conv_b, hyena_ffn_w1, hyena_ffn_b1, hyena_ffn_w2, hyena_ffn_b2, hyena_ffn_w3, hyena_sin_freq, hyena_decay, hyena_bias, diff_lambda, diff_subln_g, na_rpb, w_branch, w_out, norm_ffn_g, router_group_w, router_group_b, router_expert_w, router_expert_b, moe_w_gate, moe_w_up, moe_w_down, norm_final_g):
    batch, length, d = x.shape
    t = batch * length
    assert d == D_MODEL and length % GRID_W == 0 and t % 1024 == 0
    tq = min(512, length)
    n_tiles = 2 * t // MOE_TM + N_EXPERTS

    rope_mla = rope_tables(length, 1)
    rope_diff = rope_tables(length, 2)
    f_fwd, g_inv = dft_matrices(length)
    moe_wg = moe_w_gate.astype(BF16).reshape(DEPTH * N_EXPERTS, d, EXPERT_FF)
    moe_wu = moe_w_up.astype(BF16).reshape(DEPTH * N_EXPERTS, d, EXPERT_FF)
    moe_wd = moe_w_down.astype(BF16).reshape(DEPTH * N_EXPERTS, EXPERT_FF, d)

    x2 = x.reshape(t, d)
    h = rms_norm(x2, norm_mix_g[0], BF16)
    out = None
    for l in range(DEPTH):
        w_packed = pack_w_in(w_in[l])
        proj = matmul(h, w_packed, N_PROJ, BF16, tm=1024, tn=512, name="in_proj")

        qf = norm_matmul(proj, OFF_CQ // MLA_Q_RANK, MLA_Q_RANK, mla_q_norm_g[l], pack_w_uq(mla_w_uq[l]), BF16,
                         tm=512, name="mla_q_up")
        kv = norm_matmul(proj, OFF_CKV // MLA_KV_RANK, MLA_KV_RANK, mla_kv_norm_g[l], mla_w_ukv[l].astype(BF16), BF16,
                         tm=512, name="mla_kv_up")
        y_a = mla_attention(qf, kv, proj, rope_mla, batch, length, tq)

        coefs = hyena_filter_spectrum(length, hyena_ffn_w1[l], hyena_ffn_b1[l], hyena_ffn_w2[l], hyena_ffn_b2[l],
                                      hyena_ffn_w3[l], hyena_sin_freq[l], hyena_decay[l], f_fwd)
        y_b = hyena_mixer(proj, hyena_conv_w[l], hyena_conv_b[l], coefs, hyena_bias[l], f_fwd, g_inv, batch, length)

        lambda_init = 0.8 - 0.6 * math.exp(-0.3 * l)
        lam = diff_lambda[l].astype(F32)
        lam_full = jnp.exp(jnp.sum(lam[0] * lam[1])) - jnp.exp(jnp.sum(lam[2] * lam[3])) + lambda_init
        y_c = diff_attention(proj, rope_diff, lam_full, diff_subln_g[l], lambda_init, batch, length, tq)

        y_d = na_attention(proj, na_bias_slabs(na_rpb[l]), batch, length)

        merged = gated_merge(h, w_packed, (y_a, y_b, y_c, y_d), w_branch[l].astype(BF16))
        x_mid = matmul(merged, w_out[l].astype(BF16), d, F32, tm=1024, tn=512, res=x2, name="out_proj")

        logits, xp = router(x_mid, norm_ffn_g[l], pack_router(router_group_w[l], router_expert_w[l]))
        schedule = moe_routing(logits, router_group_b[l], router_expert_b[l], MOE_TM, n_tiles)
        out2 = moe_experts(xp, schedule, moe_wg, moe_wu, moe_wd, l, MOE_TM)
        if l + 1 < DEPTH:
            x2, h = combine_rms_norm(x_mid, out2, norm_mix_g[l + 1], BF16, emit_x=True)
        else:
            out = combine_rms_norm(x_mid, out2, norm_final_g, F32, emit_x=False)[0]
    return out.reshape(batch, length, d)
```

```python
import functools
import math

import jax
import jax.numpy as jnp
from jax import lax
from jax.experimental import pallas as pl
from jax.experimental.pallas import tpu as pltpu

F32 = jnp.float32
BF16 = jnp.bfloat16

D_MODEL = 4096
DEPTH = 2
NORM_EPS = 1e-6
ROPE_THETA = 10000.0
BRANCH_WIDTH = D_MODEL // 4
HEAD_DIM = 128
N_HEADS = BRANCH_WIDTH // HEAD_DIM
MLA_ROPE_DIM = 64
MLA_Q_RANK = (3 * D_MODEL) // 16
MLA_KV_RANK = 512
HYENA_WIDTH = BRANCH_WIDTH
HYENA_BANDS = 16
HYENA_DECAY_TARGET = 1e-2
DIFF_QK_DIM = 64
GRID_W = 64
NA_ROWS_MAX = 8
NA_COLS = 16
N_GROUPS = 4
EXPERTS_PER_GROUP = 8
N_EXPERTS = N_GROUPS * EXPERTS_PER_GROUP
EXPERT_FF = D_MODEL // 8
MASK_VALUE = -1e30

IN_CQ, IN_CKV, IN_KR, IN_HY, IN_QKV, IN_GATE, IN_END = 0, 768, 1280, 1344, 4416, 10560, 26944
OFF_QD, OFF_KD, OFF_VD = 0, 1024, 2048
OFF_QNA, OFF_KNA, OFF_VNA = 3072, 4096, 5120
N_QKV = 6144
N_HY = 3072
OFF_CQ, OFF_KR, OFF_CKV = 0, 768, 1024
N_MLA = 1536

MOE_TM = 256


def _params(semantics, vmem_mb):
    return pltpu.CompilerParams(dimension_semantics=semantics, vmem_limit_bytes=vmem_mb << 20)


def _norm(x, g):
    return x * lax.rsqrt(jnp.mean(x * x, axis=-1, keepdims=True) + NORM_EPS) * g


def _rms_kernel(x_ref, g_ref, h_ref):
    h_ref[...] = _norm(x_ref[...], g_ref[...]).astype(h_ref.dtype)


def rms_norm(x, g, out_dtype, tm=256):
    t, d = x.shape
    return pl.pallas_call(
        _rms_kernel,
        out_shape=jax.ShapeDtypeStruct((t, d), out_dtype),
        grid=(t // tm,),
        in_specs=[pl.BlockSpec((tm, d), lambda i: (i, 0)), pl.BlockSpec((1, d), lambda i: (0, 0))],
        out_specs=pl.BlockSpec((tm, d), lambda i: (i, 0)),
        compiler_params=_params(("parallel",), 40),
        name="rms_norm",
    )(x, g.reshape(1, d))


def _bf16_bits(a):
    return lax.bitcast_convert_type(a.astype(BF16).astype(F32), jnp.uint32)


def pack_halves(a):
    n = a.shape[1] // 2
    return (_bf16_bits(a[:, :n]) >> 16) | _bf16_bits(a[:, n:])


def unpack_halves(w):
    lo = lax.bitcast_convert_type(w << 16, F32)
    hi = lax.bitcast_convert_type(w & jnp.uint32(0xFFFF0000), F32)
    return jnp.concatenate([lo, hi], axis=1)


def _combine_rms_kernel(x_ref, a_ref, b_ref, g_ref, *out_refs):
    x = x_ref[...] + unpack_halves(a_ref[...]) + unpack_halves(b_ref[...])
    if len(out_refs) == 2:
        out_refs[0][...] = x
    out_refs[-1][...] = _norm(x, g_ref[...]).astype(out_refs[-1].dtype)


def combine_rms_norm(x, out2, g, out_dtype, emit_x, tm=256):
    t, d = x.shape
    nb = t // tm
    row = pl.BlockSpec((tm, d), lambda i: (i, 0))
    shapes = [jax.ShapeDtypeStruct((t, d), out_dtype)]
    if emit_x:
        shapes = [jax.ShapeDtypeStruct((t, d), F32)] + shapes
    return pl.pallas_call(
        _combine_rms_kernel,
        out_shape=shapes,
        grid=(nb,),
        in_specs=[row, pl.BlockSpec((tm, d // 2), lambda i: (i, 0)), pl.BlockSpec((tm, d // 2), lambda i: (i + nb, 0)),
                  pl.BlockSpec((1, d), lambda i: (0, 0))],
        out_specs=[row] * len(shapes),
        compiler_params=_params(("parallel",), 56),
        name="combine_rms_norm",
    )(x, out2, out2, g.reshape(1, d))


def _mm_kernel(a_ref, w_ref, o_ref):
    o_ref[...] = jnp.dot(a_ref[...], w_ref[...], preferred_element_type=F32).astype(o_ref.dtype)


def _mm_res_kernel(a_ref, w_ref, r_ref, o_ref):
    o_ref[...] = r_ref[...] + jnp.dot(a_ref[...], w_ref[...], preferred_element_type=F32)


def matmul(a, w, n_out, out_dtype, tm, tn, res=None, vmem_mb=48, name="matmul"):
    m, k = a.shape
    in_specs = [pl.BlockSpec((tm, k), lambda i, j: (i, 0)), pl.BlockSpec((k, tn), lambda i, j: (0, j))]
    args = [a, w]
    body = _mm_kernel
    if res is not None:
        in_specs.append(pl.BlockSpec((tm, tn), lambda i, j: (i, j)))
        args.append(res)
        body = _mm_res_kernel
    return pl.pallas_call(
        body,
        out_shape=jax.ShapeDtypeStruct((m, n_out), out_dtype),
        grid=(m // tm, n_out // tn),
        in_specs=in_specs,
        out_specs=pl.BlockSpec((tm, tn), lambda i, j: (i, j)),
        compiler_params=_params(("parallel", "parallel"), vmem_mb),
        name=name,
    )(*args)


def _norm_mm_kernel(a_ref, g_ref, w_ref, o_ref):
    an = _norm(a_ref[...].astype(F32), g_ref[...]).astype(BF16)
    o_ref[...] = jnp.dot(an, w_ref[...], preferred_element_type=F32).astype(o_ref.dtype)


def _router_kernel(x_ref, g_ref, w_ref, logits_ref, xp_ref):
    xn = _norm(x_ref[...], g_ref[...])
    logits_ref[...] = jnp.dot(xn.astype(BF16), w_ref[...], preferred_element_type=F32)
    xp_ref[...] = pack_halves(xn)


def router(x, g, w, tm=256):
    t, d = x.shape
    n = w.shape[1]
    return pl.pallas_call(
        _router_kernel,
        out_shape=[jax.ShapeDtypeStruct((t, n), F32), jax.ShapeDtypeStruct((t, d // 2), jnp.uint32)],
        grid=(t // tm,),
        in_specs=[pl.BlockSpec((tm, d), lambda i: (i, 0)),
                  pl.BlockSpec((1, d), lambda i: (0, 0)),
                  pl.BlockSpec((d, n), lambda i: (0, 0))],
        out_specs=[pl.BlockSpec((tm, n), lambda i: (i, 0)), pl.BlockSpec((tm, d // 2), lambda i: (i, 0))],
        compiler_params=_params(("parallel",), 40),
        name="router",
    )(x, g.reshape(1, d), w)


def norm_matmul(a, col_block, k, g, w, out_dtype, tm, vmem_mb=40, name="norm_matmul"):
    m = a.shape[0]
    n = w.shape[1]
    return pl.pallas_call(
        _norm_mm_kernel,
        out_shape=jax.ShapeDtypeStruct((m, n), out_dtype),
        grid=(m // tm,),
        in_specs=[pl.BlockSpec((tm, k), lambda i: (i, col_block)),
                  pl.BlockSpec((1, k), lambda i: (0, 0)),
                  pl.BlockSpec((k, n), lambda i: (0, 0))],
        out_specs=pl.BlockSpec((tm, n), lambda i: (i, 0)),
        compiler_params=_params(("parallel",), vmem_mb),
        name=name,
    )(a, g.reshape(1, k), w)


def _rope(x, cos_t, sin_a, sin_b):
    return x * cos_t + pltpu.roll(x, 96, 1) * sin_a + pltpu.roll(x, 32, 1) * sin_b


def rope_tables(length, chunks):
    half = MLA_ROPE_DIM // 2
    inv = 1.0 / (ROPE_THETA ** (jnp.arange(0, MLA_ROPE_DIM, 2, dtype=F32) / MLA_ROPE_DIM))
    ang = jnp.arange(length, dtype=F32)[:, None] * inv[None, :]
    cos, sin = jnp.cos(ang), jnp.sin(ang)
    zero = jnp.zeros((length, half), F32)
    c, sa, sb = [], [], []
    for i in range(2):
        on = i < chunks
        c += [cos, cos] if on else [zero, zero]
        sa += [-sin, zero] if on else [zero, zero]
        sb += [zero, sin] if on else [zero, zero]
    return jnp.concatenate(c, 1), jnp.concatenate(sa, 1), jnp.concatenate(sb, 1)


LOG2E = math.log2(math.e)


def _softmax_pv(s2, v):
    m = jnp.max(s2, axis=-1, keepdims=True)
    p = jnp.exp2(s2 - m)
    l = jnp.sum(p, axis=-1, keepdims=True)
    return jnp.dot(p.astype(BF16), v, preferred_element_type=F32) / l


def _qk(q, k):
    return lax.dot_general(q, k, (((1,), (1,)), ((), ())), preferred_element_type=F32)


def _mla_kernel(q_ref, kn_ref, v_ref, kr_ref, cq_ref, saq_ref, sbq_ref, ck_ref, sak_ref, sbk_ref, o_ref, *, scale):
    q = q_ref[...]
    c = scale * LOG2E
    qn = q[:, :HEAD_DIM].astype(F32) * c
    qr = _rope(q[:, HEAD_DIM:].astype(F32), cq_ref[...], saq_ref[...], sbq_ref[...]) * c
    qf = jnp.concatenate([qn.astype(BF16), qr.astype(BF16)], axis=1)
    kr = _rope(kr_ref[...].astype(F32), ck_ref[...], sak_ref[...], sbk_ref[...])
    kf = jnp.concatenate([kn_ref[...], kr.astype(BF16)], axis=1)
    v = v_ref[...]
    part = qf.shape[0] // 2
    for r in range(2):
        rows = slice(r * part, (r + 1) * part)
        o_ref[rows, :] = _softmax_pv(_qk(qf[rows], kf), v).astype(o_ref.dtype)


def mla_attention(qf, kv, proj, tabs, batch, length, tq):
    t = batch * length
    nq = length // tq
    scale = (HEAD_DIM + MLA_ROPE_DIM) ** -0.5
    tab_q = pl.BlockSpec((tq, HEAD_DIM), lambda b, h, i: (i, 0))
    tab_k = pl.BlockSpec((length, HEAD_DIM), lambda b, h, i: (0, 0))
    return pl.pallas_call(
        functools.partial(_mla_kernel, scale=scale),
        out_shape=jax.ShapeDtypeStruct((t, BRANCH_WIDTH), BF16),
        grid=(batch, N_HEADS, nq),
        in_specs=[pl.BlockSpec((tq, 2 * HEAD_DIM), lambda b, h, i: (b * nq + i, h)),
                  pl.BlockSpec((length, HEAD_DIM), lambda b, h, i: (b, 2 * h)),
                  pl.BlockSpec((length, HEAD_DIM), lambda b, h, i: (b, 2 * h + 1)),
                  pl.BlockSpec((length, HEAD_DIM), lambda b, h, i: (b, OFF_KR // HEAD_DIM)),
                  tab_q, tab_q, tab_q, tab_k, tab_k, tab_k],
        out_specs=pl.BlockSpec((tq, HEAD_DIM), lambda b, h, i: (b * nq + i, h)),
        compiler_params=_params(("parallel", "parallel", "parallel"), 48),
        name="mla_attention",
    )(qf, kv, kv, proj, *tabs, *tabs)


def _diff_kernel(q_ref, k_ref, v_ref, cq_ref, saq_ref, sbq_ref, ck_ref, sak_ref, sbk_ref, lam_ref, g_ref, o_ref,
                 *, scale, out_scale):
    q = _rope(q_ref[...].astype(F32), cq_ref[...], saq_ref[...], sbq_ref[...])
    q = q * (scale * LOG2E)
    k = _rope(k_ref[...].astype(F32), ck_ref[...], sak_ref[...], sbk_ref[...]).astype(BF16)
    lane = lax.broadcasted_iota(jnp.int32, q.shape, 1)
    q0 = jnp.where(lane < DIFF_QK_DIM, q, 0.0).astype(BF16)
    q1 = jnp.where(lane >= DIFF_QK_DIM, q, 0.0).astype(BF16)
    v = v_ref[...]
    half = q.shape[0] // 2
    for r in range(2):
        rows = slice(r * half, (r + 1) * half)
        o = _softmax_pv(_qk(q0[rows], k), v) - lam_ref[...] * _softmax_pv(_qk(q1[rows], k), v)
        o_ref[rows, :] = (_norm(o, g_ref[...]) * out_scale).astype(o_ref.dtype)


def diff_attention(proj, tabs, lam_full, subln_g, lambda_init, batch, length, tq):
    t = batch * length
    nq = length // tq
    tab_q = pl.BlockSpec((tq, HEAD_DIM), lambda b, h, i: (i, 0))
    tab_k = pl.BlockSpec((length, HEAD_DIM), lambda b, h, i: (0, 0))
    vec = pl.BlockSpec((1, HEAD_DIM), lambda b, h, i: (0, 0))
    return pl.pallas_call(
        functools.partial(_diff_kernel, scale=DIFF_QK_DIM ** -0.5, out_scale=1.0 - lambda_init),
        out_shape=jax.ShapeDtypeStruct((t, BRANCH_WIDTH), BF16),
        grid=(batch, N_HEADS, nq),
        in_specs=[pl.BlockSpec((tq, HEAD_DIM), lambda b, h, i: (b * nq + i, OFF_QD // HEAD_DIM + h)),
                  pl.BlockSpec((length, HEAD_DIM), lambda b, h, i: (b, OFF_KD // HEAD_DIM + h)),
                  pl.BlockSpec((length, HEAD_DIM), lambda b, h, i: (b, OFF_VD // HEAD_DIM + h)),
                  tab_q, tab_q, tab_q, tab_k, tab_k, tab_k, vec, vec],
        out_specs=pl.BlockSpec((tq, HEAD_DIM), lambda b, h, i: (b * nq + i, h)),
        compiler_params=_params(("parallel", "parallel", "parallel"), 48),
        name="diff_attention",
    )(proj, proj, proj, *tabs, *tabs, jnp.full((1, HEAD_DIM), lam_full, F32), subln_g.reshape(1, HEAD_DIM))


NA_GROUP = 4


def _na_kernel(q_ref, k_ref, v_ref, cb_ref, o_ref, *, rows, kr, win, scale):
    g = pl.program_id(1)
    w_start = jnp.clip(g * NA_GROUP - kr // 2, 0, rows - win)
    base = pl.multiple_of(w_start * GRID_W, GRID_W)
    nk = win * GRID_W
    left = lax.broadcasted_iota(jnp.int32, (GRID_W, 2 * GRID_W), 1) < GRID_W

    slab, row_mask = [], []
    for j in range(NA_GROUP):
        r = g * NA_GROUP + j
        r_start = jnp.clip(r - kr // 2, 0, rows - kr)
        slab_j, mask_j = [], []
        for p in range(win // 2):
            a = w_start + 2 * p
            ok_l = (a >= r_start) & (a < r_start + kr)
            ok_r = (a + 1 >= r_start) & (a + 1 < r_start + kr)
            slab_j.append(jnp.clip(a - r + NA_ROWS_MAX, 0, 2 * NA_ROWS_MAX - 1))
            mask_j.append(jnp.where(left, jnp.where(ok_l, 0.0, MASK_VALUE), jnp.where(ok_r, 0.0, MASK_VALUE)))
        slab.append(slab_j)
        row_mask.append(jnp.concatenate(mask_j, axis=1))

    outs = []
    for h in range(N_HEADS):
        cols = slice(h * HEAD_DIM, (h + 1) * HEAD_DIM)
        qh = (q_ref[:, cols].astype(F32) * (scale * LOG2E)).astype(BF16)
        bias = jnp.concatenate(
            [jnp.concatenate([cb_ref[h, slab[j][p]] for p in range(win // 2)], axis=1) + row_mask[j]
             for j in range(NA_GROUP)], axis=0)
        s2 = _qk(qh, k_ref[pl.ds(base, nk), cols]) + bias
        outs.append(_softmax_pv(s2, v_ref[pl.ds(base, nk), cols]))
    o_ref[...] = jnp.concatenate(outs, axis=1).astype(o_ref.dtype)


def na_bias_slabs(rpb):
    cols = jnp.arange(GRID_W)
    col_start = jnp.clip(cols - NA_COLS // 2, 0, GRID_W - NA_COLS)
    col_in = (cols[None, :] >= col_start[:, None]) & (cols[None, :] < col_start[:, None] + NA_COLS)
    col_idx = jnp.clip(cols[None, :] - cols[:, None], -(NA_COLS - 1), NA_COLS - 1) + NA_COLS - 1
    col_bias = jnp.where(col_in[None, None], rpb[:, :, col_idx].astype(F32) * LOG2E, MASK_VALUE)
    pad = jnp.full((N_HEADS, 1, GRID_W, GRID_W), MASK_VALUE, F32)
    ext = jnp.concatenate([pad, col_bias, pad], axis=1)
    return jnp.concatenate([ext[:, :-1], ext[:, 1:]], axis=-1)


def na_attention(proj, slabs, batch, length):
    t = batch * length
    rows = length // GRID_W
    kr = min(NA_ROWS_MAX, rows)
    win = min(rows, kr + NA_GROUP)
    assert win % 2 == 0 and rows % NA_GROUP == 0
    steps = rows // NA_GROUP
    kv_spec = lambda off: pl.BlockSpec((length, BRANCH_WIDTH), lambda b, r: (b, off // BRANCH_WIDTH))
    return pl.pallas_call(
        functools.partial(_na_kernel, rows=rows, kr=kr, win=win, scale=HEAD_DIM ** -0.5),
        out_shape=jax.ShapeDtypeStruct((t, BRANCH_WIDTH), BF16),
        grid=(batch, steps),
        in_specs=[pl.BlockSpec((NA_GROUP * GRID_W, BRANCH_WIDTH),
                               lambda b, r: (b * steps + r, OFF_QNA // BRANCH_WIDTH)),
                  kv_spec(OFF_KNA), kv_spec(OFF_VNA),
                  pl.BlockSpec((N_HEADS, 2 * NA_ROWS_MAX, GRID_W, 2 * GRID_W), lambda b, r: (0, 0, 0, 0))],
        out_specs=pl.BlockSpec((NA_GROUP * GRID_W, BRANCH_WIDTH), lambda b, r: (b * steps + r, 0)),
        compiler_params=_params(("parallel", "parallel"), 48),
        name="na_attention",
    )(proj, proj, proj, slabs)


HY_TC = 1024
CONV_TC = 512


def _short_conv_kernel(u_ref, w_ref, b_ref, o_ref):
    u = u_ref[...].astype(F32)
    n = u.shape[0]
    row = lax.broadcasted_iota(jnp.int32, u.shape, 0)
    prev = jnp.where(row == 0, 0.0, pltpu.roll(u, 1, 0))
    nxt = jnp.where(row == n - 1, 0.0, pltpu.roll(u, n - 1, 0))
    w = w_ref[...]
    o_ref[...] = (prev * w[0:1] + u * w[1:2] + nxt * w[2:3] + b_ref[...]).astype(o_ref.dtype)


def short_conv(proj, w, bias, batch, length):
    t = batch * length
    width = 3 * HYENA_WIDTH
    return pl.pallas_call(
        _short_conv_kernel,
        out_shape=jax.ShapeDtypeStruct((t, width), BF16),
        grid=(batch, width // CONV_TC),
        in_specs=[pl.BlockSpec((length, CONV_TC), lambda b, c: (b, c)),
                  pl.BlockSpec((3, CONV_TC), lambda b, c: (0, c)),
                  pl.BlockSpec((1, CONV_TC), lambda b, c: (0, c))],
        out_specs=pl.BlockSpec((length, CONV_TC), lambda b, c: (b, c)),
        compiler_params=_params(("parallel", "parallel"), 48),
        name="short_conv",
    )(proj, w, bias.reshape(1, width))


def dft_matrices(length):
    n = 2 * length
    kb = min(512, length)
    nkb = length // kb
    k = jnp.arange(length, dtype=jnp.int32)[:, None]
    s = jnp.arange(length, dtype=jnp.int32)[None, :]
    ang = ((k * s) % n).astype(F32) * (2.0 * math.pi / n)
    c, sn = jnp.cos(ang), jnp.sin(ang)
    alt_s = jnp.where(s % 2 == 0, 1.0, -1.0).astype(F32)
    alt_t = jnp.where(k % 2 == 0, 1.0, -1.0).astype(F32)
    f_re = c.astype(BF16)
    f_im = jnp.where(k == 0, alt_s, -sn).astype(BF16)
    g_c = jnp.where(s == 0, 1.0 / n, c * (2.0 / n)).astype(BF16)
    g_s = jnp.where(s == 0, alt_t / n, -sn * (2.0 / n)).astype(BF16)
    blk = lambda i: slice(i * kb, (i + 1) * kb)
    f_fwd = jnp.concatenate([m[blk(i)] for i in range(nkb) for m in (f_re, f_im)], axis=0)
    g_inv = jnp.concatenate([m[:, blk(i)] for i in range(nkb) for m in (g_c, g_s)], axis=1)
    return f_fwd, g_inv


def hyena_filter_spectrum(length, w1, b1, w2, b2, w3, freq, decay, f_fwd):
    hp = lax.Precision.HIGHEST
    pos = jnp.arange(length, dtype=F32)
    tt = (pos / max(length - 1, 1))[:, None]
    bands = jnp.linspace(1e-4, HYENA_BANDS - 1, HYENA_BANDS, dtype=F32)
    ang = (2.0 * math.pi / length) * pos[:, None] * bands[None, :]
    feats = jnp.concatenate([tt, jnp.cos(ang), jnp.sin(ang)], axis=-1)
    h = jnp.sin(freq[0] * (jnp.dot(feats, w1, precision=hp) + b1))
    h = jnp.sin(freq[1] * (jnp.dot(h, w2, precision=hp) + b2))
    h = jnp.dot(h, w3, precision=hp).reshape(length, 2, 2, HYENA_WIDTH)
    h = h * jnp.exp(-tt[:, :, None, None] * jnp.abs(decay))
    width = 2 * HYENA_WIDTH
    fwd = h[:, :, 0].reshape(length, width)
    bwd = jnp.where(pos[:, None] == 0, 0.0, h[:, :, 1].reshape(length, width))
    inv_l1 = 1.0 / (jnp.sum(jnp.abs(fwd), axis=0) + jnp.sum(jnp.abs(bwd), axis=0))
    n = 2 * length
    kb = min(512, length)
    spec = matmul(f_fwd, jnp.concatenate([fwd, bwd], axis=1).astype(BF16), 2 * width, F32,
                  tm=min(1024, n), tn=512, name="filter_dft")
    re_rows = lambda i: slice(2 * i * kb, (2 * i + 1) * kb)
    im_rows = lambda i: slice((2 * i + 1) * kb, (2 * i + 2) * kb)
    nkb = length // kb
    k_re = jnp.concatenate([spec[re_rows(i), :width] + spec[re_rows(i), width:] for i in range(nkb)], axis=0) * inv_l1
    k_im = jnp.concatenate([spec[im_rows(i), :width] - spec[im_rows(i), width:] for i in range(nkb)], axis=0) * inv_l1
    nyquist = (spec[kb, :width] + spec[kb, width:]) * inv_l1
    first = jnp.arange(length)[:, None] == 0
    return k_re, jnp.where(first, 0.0, k_im), jnp.where(first, nyquist[None, :], k_re)


def _dft_fwd_kernel(f_ref, z_ref, a_ref, b_ref, a2_ref, o_ref):
    res = jnp.dot(f_ref[...], z_ref[...], preferred_element_type=F32)
    kb = res.shape[0] // 2
    z_re, z_im = res[:kb], res[kb:]
    b = b_ref[...]
    o_ref[:kb, :] = (z_re * a_ref[...] - z_im * b).astype(o_ref.dtype)
    o_ref[kb:, :] = (z_re * b + z_im * a2_ref[...]).astype(o_ref.dtype)


def dft_forward(f_fwd, z, z_col, coefs, order, batch, length):
    n = 2 * length
    kb = min(512, length)
    nkb = length // kb
    nc = HYENA_WIDTH // HY_TC
    coef = pl.BlockSpec((kb, HY_TC), lambda b, c, k: (k, order * nc + c))
    return pl.pallas_call(
        _dft_fwd_kernel,
        out_shape=jax.ShapeDtypeStruct((batch * n, HYENA_WIDTH), BF16),
        grid=(batch, nc, nkb),
        in_specs=[pl.BlockSpec((2 * kb, length), lambda b, c, k: (k, 0)),
                  pl.BlockSpec((length, HY_TC), lambda b, c, k: (b, z_col + c)),
                  coef, coef, coef],
        out_specs=pl.BlockSpec((2 * kb, HY_TC), lambda b, c, k: (b * nkb + k, c)),
        compiler_params=_params(("parallel", "parallel", "parallel"), 56),
        name="dft_forward",
    )(f_fwd, z, *coefs)


def _dft_inv_kernel(g_ref, y_ref, x_ref, z_ref, bias_ref, o_ref):
    y = jnp.dot(g_ref[...], y_ref[...], preferred_element_type=F32)
    o_ref[...] = (x_ref[...].astype(F32) * (y + bias_ref[...] * z_ref[...].astype(F32))).astype(o_ref.dtype)


def dft_inverse(g_inv, spec, gate, gate_col, z, z_col, bias, batch, length):
    n = 2 * length
    tt = min(1024, length)
    nt = length // tt
    nc = HYENA_WIDTH // HY_TC
    return pl.pallas_call(
        _dft_inv_kernel,
        out_shape=jax.ShapeDtypeStruct((batch * length, HYENA_WIDTH), BF16),
        grid=(batch, nc, nt),
        in_specs=[pl.BlockSpec((tt, n), lambda b, c, i: (i, 0)),
                  pl.BlockSpec((n, HY_TC), lambda b, c, i: (b, c)),
                  pl.BlockSpec((tt, HY_TC), lambda b, c, i: (b * nt + i, gate_col + c)),
                  pl.BlockSpec((tt, HY_TC), lambda b, c, i: (b * nt + i, z_col + c)),
                  pl.BlockSpec((1, HY_TC), lambda b, c, i: (0, c))],
        out_specs=pl.BlockSpec((tt, HY_TC), lambda b, c, i: (b * nt + i, c)),
        compiler_params=_params(("parallel", "parallel", "parallel"), 56),
        name="dft_inverse",
    )(g_inv, spec, gate, z, bias.reshape(1, HYENA_WIDTH))


def hyena_mixer(proj, conv_w, conv_b, coefs, bias, f_fwd, g_inv, batch, length):
    nc = HYENA_WIDTH // HY_TC
    cv = short_conv(proj, conv_w, conv_b, batch, length)
    spec = dft_forward(f_fwd, cv, 0, coefs, 0, batch, length)
    z1 = dft_inverse(g_inv, spec, cv, nc, cv, 0, bias[0], batch, length)
    spec = dft_forward(f_fwd, z1, 0, coefs, 1, batch, length)
    return dft_inverse(g_inv, spec, cv, 2 * nc, z1, 0, bias[1], batch, length)


def _merge_kernel(h_ref, g0, g1, g2, g3, y0, y1, y2, y3, wb_ref, o_ref):
    h = h_ref[...]
    acc = None
    for i, (g_ref, y_ref) in enumerate(((g0, y0), (g1, y1), (g2, y2), (g3, y3))):
        gate = jax.nn.sigmoid(jnp.dot(h, g_ref[...], preferred_element_type=F32))
        term = gate * jnp.dot(y_ref[...], wb_ref[i], preferred_element_type=F32)
        acc = term if acc is None else acc + term
    o_ref[...] = acc.astype(o_ref.dtype)


def gated_merge(h, w_gate, ys, w_branch, tm=512, tn=256):
    t, d = h.shape
    gate_spec = lambda i: pl.BlockSpec((d, tn), lambda n, m: (0, (i * d) // tn + n))
    y_spec = pl.BlockSpec((tm, BRANCH_WIDTH), lambda n, m: (m, 0))
    return pl.pallas_call(
        _merge_kernel,
        out_shape=jax.ShapeDtypeStruct((t, d), BF16),
        grid=(d // tn, t // tm),
        in_specs=[pl.BlockSpec((tm, d), lambda n, m: (m, 0))] + [gate_spec(i) for i in range(4)] + [y_spec] * 4
                 + [pl.BlockSpec((4, BRANCH_WIDTH, tn), lambda n, m: (0, 0, n))],
        out_specs=pl.BlockSpec((tm, tn), lambda n, m: (m, n)),
        compiler_params=_params(("parallel", "parallel"), 52),
        name="gated_merge",
    )(h, w_gate, w_gate, w_gate, w_gate, *ys, w_branch)


def moe_routing(logits, rg_b, re_b, tm, n_tiles):
    t = logits.shape[0]
    g_prob = jax.nn.softmax(logits[:, :N_GROUPS] + rg_b, axis=-1)
    g_idx = jnp.argmax(g_prob, axis=-1)
    g_val = jnp.max(g_prob, axis=-1)
    e_logit = (logits[:, N_GROUPS:N_GROUPS + N_EXPERTS] + re_b).reshape(t, N_GROUPS, EXPERTS_PER_GROUP)
    e_sel = jnp.take_along_axis(e_logit, g_idx[:, None, None], axis=1)[:, 0]
    top_val, top_idx = lax.top_k(jax.nn.softmax(e_sel, axis=-1), 2)
    top_val = top_val / jnp.sum(top_val, axis=-1, keepdims=True)
    weight = (g_val[:, None] * top_val).T.reshape(-1)
    expert = (g_idx[:, None] * EXPERTS_PER_GROUP + top_idx).T.reshape(-1).astype(jnp.int32)

    order = jnp.argsort(expert, stable=True).astype(jnp.int32)
    counts = jnp.sum(expert[None, :] == jnp.arange(N_EXPERTS, dtype=jnp.int32)[:, None], axis=1, dtype=jnp.int32)
    padded = ((counts + tm - 1) // tm) * tm
    pad_end = jnp.cumsum(padded)
    pad_start = pad_end - padded
    start = jnp.cumsum(counts) - counts
    tile_start = jnp.arange(n_tiles, dtype=jnp.int32) * tm
    tile_expert = jnp.minimum(jnp.sum(pad_end[None, :] <= tile_start[:, None], axis=1, dtype=jnp.int32), N_EXPERTS - 1)
    n_used = (pad_end[-1] // tm).astype(jnp.int32).reshape(1)
    in_tile = jnp.arange(tm, dtype=jnp.int32)[None, :]
    off = (tile_start - pad_start[tile_expert])[:, None] + in_tile
    valid = (off < counts[tile_expert][:, None]) & (tile_start < pad_end[-1])[:, None]
    src = order[jnp.clip(start[tile_expert][:, None] + off, 0, 2 * t - 1)]
    row_tok = jnp.where(valid, src % t, 0).astype(jnp.int32)
    row_dst = jnp.where(valid, src, 2 * t + in_tile).astype(jnp.int32)
    row_w = jnp.where(valid, weight[src], 0.0).astype(F32)
    return tile_expert, n_used, row_tok, row_dst, row_w


MOE_UNROLL = 32


def _moe_kernel(te_ref, nu_ref, tok_ref, tok_next_ref, dst_ref, w_ref, x_hbm, wg_ref, wu_ref, wd_ref, o_hbm,
                xbuf, obuf, gsem, ssem):
    tm = xbuf.shape[1]
    i = pl.program_id(0)
    n_used = nu_ref[0]
    slot = i % 2

    def issue_gather(idx_ref, s):
        @pl.loop(0, tm // MOE_UNROLL)
        def _(c):
            for j in range(MOE_UNROLL):
                r = c * MOE_UNROLL + j
                pltpu.make_async_copy(x_hbm.at[pl.ds(idx_ref[0, 0, r], 1)], xbuf.at[s, pl.ds(r, 1)],
                                      gsem.at[s]).start()

    def wait_gather(s):
        pltpu.make_async_copy(x_hbm.at[pl.ds(0, tm)], xbuf.at[s], gsem.at[s]).wait()

    def wait_scatter(s):
        pltpu.make_async_copy(obuf.at[s], o_hbm.at[pl.ds(0, tm)], ssem.at[s]).wait()

    @pl.when(i < n_used)
    def _():
        @pl.when(i == 0)
        def _():
            issue_gather(tok_ref, slot)
            obuf[1] = jnp.zeros(obuf.shape[1:], obuf.dtype)
            init = pltpu.make_async_copy(obuf.at[1], o_hbm.at[pl.ds(o_hbm.shape[0] - tm, tm)], ssem.at[1])
            init.start()
            init.wait()

        wait_gather(slot)
        xn = unpack_halves(xbuf[slot]).astype(BF16)
        issue_gather(tok_next_ref, 1 - slot)
        a = jnp.dot(xn, wg_ref[0], preferred_element_type=F32)
        u = jnp.dot(xn, wu_ref[0], preferred_element_type=F32)
        hid = (a * jax.nn.sigmoid(a) * u * w_ref[...]).astype(BF16)
        res = pack_halves(jnp.dot(hid, wd_ref[0], preferred_element_type=F32))

        @pl.when(i >= 2)
        def _():
            wait_scatter(slot)

        obuf[slot] = res

        @pl.loop(0, tm // MOE_UNROLL)
        def _(c):
            for j in range(MOE_UNROLL):
                r = c * MOE_UNROLL + j
                pltpu.make_async_copy(obuf.at[slot, pl.ds(r, 1)], o_hbm.at[pl.ds(dst_ref[0, 0, r], 1)],
                                      ssem.at[slot]).start()

        @pl.when(i == n_used - 1)
        def _():
            wait_gather(1 - slot)
            wait_scatter(slot)

            @pl.when(i >= 1)
            def _():
                wait_scatter(1 - slot)


def moe_experts(xp, schedule, w_gate, w_up, w_down, layer, tm):
    t, dp = xp.shape
    d = 2 * dp
    tile_expert, n_used, row_tok, row_dst, row_w = schedule
    n_tiles = tile_expert.shape[0]
    w_idx = lambda i, te, nu: (layer * N_EXPERTS + te[i], 0, 0)
    idx_spec = lambda off: pl.BlockSpec((1, 1, tm), lambda i, te, nu: (jnp.minimum(i + off, n_tiles - 1), 0, 0),
                                        memory_space=pltpu.SMEM)
    grid_spec = pltpu.PrefetchScalarGridSpec(
        num_scalar_prefetch=2,
        grid=(n_tiles,),
        in_specs=[idx_spec(0), idx_spec(1), idx_spec(0),
                  pl.BlockSpec((tm, 1), lambda i, te, nu: (i, 0)),
                  pl.BlockSpec(memory_space=pl.ANY),
                  pl.BlockSpec((1, d, EXPERT_FF), w_idx),
                  pl.BlockSpec((1, d, EXPERT_FF), w_idx),
                  pl.BlockSpec((1, EXPERT_FF, d), w_idx)],
        out_specs=pl.BlockSpec(memory_space=pl.ANY),
        scratch_shapes=[pltpu.VMEM((2, tm, dp), jnp.uint32), pltpu.VMEM((2, tm, dp), jnp.uint32),
                        pltpu.SemaphoreType.DMA((2,)), pltpu.SemaphoreType.DMA((2,))],
    )
    tok3 = row_tok.reshape(n_tiles, 1, tm)
    return pl.pallas_call(
        _moe_kernel,
        out_shape=jax.ShapeDtypeStruct((2 * t + tm, dp), jnp.uint32),
        grid_spec=grid_spec,
        compiler_params=_params(("arbitrary",), 48),
        name="moe_experts",
    )(tile_expert, n_used, tok3, tok3, row_dst.reshape(n_tiles, 1, tm),
      row_w.reshape(n_tiles * tm, 1), xp, w_gate, w_up, w_down)


def split_w_in(w):
    d = w.shape[0]
    w = w.astype(BF16)
    z = lambda n: jnp.zeros((d, n), BF16)
    w_mla = jnp.concatenate([w[:, IN_CQ:IN_CKV], w[:, IN_KR:IN_HY], z(64), z(128), w[:, IN_CKV:IN_KR]], axis=1)
    return w[:, IN_QKV:IN_GATE], w[:, IN_HY:IN_QKV], w_mla, w[:, IN_GATE:IN_END]


def pack_w_uq(w):
    r = w.shape[0]
    w = w.reshape(r, N_HEADS, HEAD_DIM + MLA_ROPE_DIM)
    w = jnp.concatenate([w, jnp.zeros((r, N_HEADS, HEAD_DIM - MLA_ROPE_DIM), w.dtype)], axis=-1)
    return w.reshape(r, N_HEADS * 2 * HEAD_DIM).astype(BF16)


def pack_router(rg_w, re_w):
    d = rg_w.shape[0]
    pad = jnp.zeros((d, HEAD_DIM - N_GROUPS - N_EXPERTS), rg_w.dtype)
    return jnp.concatenate([rg_w, re_w, pad], axis=1).astype(BF16)


def kernel(x, norm_mix_g, w_in, mla_q_norm_g, mla_kv_norm_g, mla_w_uq, mla_w_ukv, hyena_conv_w, hyena_conv_b, hyena_ffn_w1, hyena_ffn_b1, hyena_ffn_w2, hyena_ffn_b2, hyena_ffn_w3, hyena_sin_freq, hyena_decay, hyena_bias, diff_lambda, diff_subln_g, na_rpb, w_branch, w_out, norm_ffn_g, router_group_w, router_group_b, router_expert_w, router_expert_b, moe_w_gate, moe_w_up, moe_w_down, norm_final_g):
    batch, length, d = x.shape
    t = batch * length
    assert d == D_MODEL and length % GRID_W == 0 and t % 1024 == 0
    tq = min(512, length)
    n_tiles = 2 * t // MOE_TM + N_EXPERTS

    rope_mla = rope_tables(length, 1)
    rope_diff = rope_tables(length, 2)
    f_fwd, g_inv = dft_matrices(length)
    moe_wg = moe_w_gate.astype(BF16).reshape(DEPTH * N_EXPERTS, d, EXPERT_FF)
    moe_wu = moe_w_up.astype(BF16).reshape(DEPTH * N_EXPERTS, d, EXPERT_FF)
    moe_wd = moe_w_down.astype(BF16).reshape(DEPTH * N_EXPERTS, EXPERT_FF, d)

    x2 = x.reshape(t, d)
    h = rms_norm(x2, norm_mix_g[0], BF16)
    out = None
    for l in range(DEPTH):
        w_qkv, w_hy, w_mla, w_gate = split_w_in(w_in[l])
        proj_qkv = matmul(h, w_qkv, N_QKV, BF16, tm=1024, tn=512, name="in_proj_qkv")
        proj_hy = matmul(h, w_hy, N_HY, BF16, tm=1024, tn=512, name="in_proj_hy")
        proj_mla = matmul(h, w_mla, N_MLA, BF16, tm=1024, tn=512, name="in_proj_mla")

        qf = norm_matmul(proj_mla, OFF_CQ // MLA_Q_RANK, MLA_Q_RANK, mla_q_norm_g[l], pack_w_uq(mla_w_uq[l]), BF16,
                         tm=512, name="mla_q_up")
        kv = norm_matmul(proj_mla, OFF_CKV // MLA_KV_RANK, MLA_KV_RANK, mla_kv_norm_g[l], mla_w_ukv[l].astype(BF16),
                         BF16, tm=512, name="mla_kv_up")
        y_a = mla_attention(qf, kv, proj_mla, rope_mla, batch, length, tq)

        coefs = hyena_filter_spectrum(length, hyena_ffn_w1[l], hyena_ffn_b1[l], hyena_ffn_w2[l], hyena_ffn_b2[l],
                                      hyena_ffn_w3[l], hyena_sin_freq[l], hyena_decay[l], f_fwd)
        y_b = hyena_mixer(proj_hy, hyena_conv_w[l], hyena_conv_b[l], coefs, hyena_bias[l], f_fwd, g_inv, batch, length)

        lambda_init = 0.8 - 0.6 * math.exp(-0.3 * l)
        lam = diff_lambda[l].astype(F32)
        lam_full = jnp.exp(jnp.sum(lam[0] * lam[1])) - jnp.exp(jnp.sum(lam[2] * lam[3])) + lambda_init
        y_c = diff_attention(proj_qkv, rope_diff, lam_full, diff_subln_g[l], lambda_init, batch, length, tq)

        y_d = na_attention(proj_qkv, na_bias_slabs(na_rpb[l]), batch, length)

        merged = gated_merge(h, w_gate, (y_a, y_b, y_c, y_d), w_branch[l].astype(BF16))
        x_mid = matmul(merged, w_out[l].astype(BF16), d, F32, tm=1024, tn=512, res=x2, name="out_proj")

        logits, xp = router(x_mid, norm_ffn_g[l], pack_router(router_group_w[l], router_expert_w[l]))
        schedule = moe_routing(logits, router_group_b[l], router_expert_b[l], MOE_TM, n_tiles)
        out2 = moe_experts(xp, schedule, moe_wg, moe_wu, moe_wd, l, MOE_TM)
        if l + 1 < DEPTH:
            x2, h = combine_rms_norm(x_mid, out2, norm_mix_g[l + 1], BF16, emit_x=True)
        else:
            out = combine_rms_norm(x_mid, out2, norm_final_g, F32, emit_x=False)[0]
    return out.reshape(batch, length, d)
```

```python
import functools
import math

import jax
import jax.numpy as jnp
from jax import lax
from jax.experimental import pallas as pl
from jax.experimental.pallas import tpu as pltpu

F32 = jnp.float32
BF16 = jnp.bfloat16

D_MODEL = 4096
DEPTH = 2
NORM_EPS = 1e-6
ROPE_THETA = 10000.0
BRANCH_WIDTH = D_MODEL // 4
HEAD_DIM = 128
N_HEADS = BRANCH_WIDTH // HEAD_DIM
MLA_ROPE_DIM = 64
MLA_Q_RANK = (3 * D_MODEL) // 16
MLA_KV_RANK = 512
HYENA_WIDTH = BRANCH_WIDTH
HYENA_BANDS = 16
HYENA_DECAY_TARGET = 1e-2
DIFF_QK_DIM = 64
GRID_W = 64
NA_ROWS_MAX = 8
NA_COLS = 16
N_GROUPS = 4
EXPERTS_PER_GROUP = 8
N_EXPERTS = N_GROUPS * EXPERTS_PER_GROUP
EXPERT_FF = D_MODEL // 8
MASK_VALUE = -1e30

IN_CQ, IN_CKV, IN_KR, IN_HY, IN_QKV, IN_GATE, IN_END = 0, 768, 1280, 1344, 4416, 10560, 26944
OFF_QD, OFF_KD, OFF_VD = 0, 1024, 2048
OFF_QNA, OFF_KNA, OFF_VNA = 3072, 4096, 5120
N_QKV = 6144
N_HY = 3072
OFF_CQ, OFF_KR, OFF_CKV = 0, 768, 1024
N_MLA = 1536

MOE_TM = 256


def _params(semantics, vmem_mb):
    return pltpu.CompilerParams(dimension_semantics=semantics, vmem_limit_bytes=vmem_mb << 20)


def _norm(x, g):
    return x * lax.rsqrt(jnp.mean(x * x, axis=-1, keepdims=True) + NORM_EPS) * g


def _rms_kernel(x_ref, g_ref, h_ref):
    h_ref[...] = _norm(x_ref[...], g_ref[...]).astype(h_ref.dtype)


def rms_norm(x, g, out_dtype, tm=256):
    t, d = x.shape
    return pl.pallas_call(
        _rms_kernel,
        out_shape=jax.ShapeDtypeStruct((t, d), out_dtype),
        grid=(t // tm,),
        in_specs=[pl.BlockSpec((tm, d), lambda i: (i, 0)), pl.BlockSpec((1, d), lambda i: (0, 0))],
        out_specs=pl.BlockSpec((tm, d), lambda i: (i, 0)),
        compiler_params=_params(("parallel",), 40),
        name="rms_norm",
    )(x, g.reshape(1, d))


def _bf16_bits(a):
    return lax.bitcast_convert_type(a.astype(BF16).astype(F32), jnp.uint32)


def pack_halves(a):
    n = a.shape[1] // 2
    return (_bf16_bits(a[:, :n]) >> 16) | _bf16_bits(a[:, n:])


def unpack_halves(w):
    lo = lax.bitcast_convert_type(w << 16, F32)
    hi = lax.bitcast_convert_type(w & jnp.uint32(0xFFFF0000), F32)
    return jnp.concatenate([lo, hi], axis=1)


def _combine_rms_kernel(x_ref, a_ref, b_ref, g_ref, *out_refs):
    x = x_ref[...] + unpack_halves(a_ref[...]) + unpack_halves(b_ref[...])
    if len(out_refs) == 2:
        out_refs[0][...] = x
    out_refs[-1][...] = _norm(x, g_ref[...]).astype(out_refs[-1].dtype)


def combine_rms_norm(x, out2, g, out_dtype, emit_x, tm=256):
    t, d = x.shape
    nb = t // tm
    row = pl.BlockSpec((tm, d), lambda i: (i, 0))
    shapes = [jax.ShapeDtypeStruct((t, d), out_dtype)]
    if emit_x:
        shapes = [jax.ShapeDtypeStruct((t, d), F32)] + shapes
    return pl.pallas_call(
        _combine_rms_kernel,
        out_shape=shapes,
        grid=(nb,),
        in_specs=[row, pl.BlockSpec((tm, d // 2), lambda i: (i, 0)), pl.BlockSpec((tm, d // 2), lambda i: (i + nb, 0)),
                  pl.BlockSpec((1, d), lambda i: (0, 0))],
        out_specs=[row] * len(shapes),
        compiler_params=_params(("parallel",), 56),
        name="combine_rms_norm",
    )(x, out2, out2, g.reshape(1, d))


def _mm_kernel(a_ref, w_ref, o_ref):
    o_ref[...] = jnp.dot(a_ref[...], w_ref[...], preferred_element_type=F32).astype(o_ref.dtype)


def _mm_res_kernel(a_ref, w_ref, r_ref, o_ref):
    o_ref[...] = r_ref[...] + jnp.dot(a_ref[...], w_ref[...], preferred_element_type=F32)


def matmul(a, w, n_out, out_dtype, tm, tn, res=None, vmem_mb=48, name="matmul"):
    m, k = a.shape
    in_specs = [pl.BlockSpec((tm, k), lambda i, j: (i, 0)), pl.BlockSpec((k, tn), lambda i, j: (0, j))]
    args = [a, w]
    body = _mm_kernel
    if res is not None:
        in_specs.append(pl.BlockSpec((tm, tn), lambda i, j: (i, j)))
        args.append(res)
        body = _mm_res_kernel
    return pl.pallas_call(
        body,
        out_shape=jax.ShapeDtypeStruct((m, n_out), out_dtype),
        grid=(m // tm, n_out // tn),
        in_specs=in_specs,
        out_specs=pl.BlockSpec((tm, tn), lambda i, j: (i, j)),
        compiler_params=_params(("parallel", "parallel"), vmem_mb),
        name=name,
    )(*args)


def _norm_mm_kernel(a_ref, g_ref, w_ref, o_ref):
    an = _norm(a_ref[...].astype(F32), g_ref[...]).astype(BF16)
    o_ref[...] = jnp.dot(an, w_ref[...], preferred_element_type=F32).astype(o_ref.dtype)


def _router_kernel(x_ref, g_ref, w_ref, logits_ref, xp_ref):
    xn = _norm(x_ref[...], g_ref[...])
    logits_ref[...] = jnp.dot(xn.astype(BF16), w_ref[...], preferred_element_type=F32)
    xp_ref[...] = pack_halves(xn)


def router(x, g, w, tm=256):
    t, d = x.shape
    n = w.shape[1]
    return pl.pallas_call(
        _router_kernel,
        out_shape=[jax.ShapeDtypeStruct((t, n), F32), jax.ShapeDtypeStruct((t, d // 2), jnp.uint32)],
        grid=(t // tm,),
        in_specs=[pl.BlockSpec((tm, d), lambda i: (i, 0)),
                  pl.BlockSpec((1, d), lambda i: (0, 0)),
                  pl.BlockSpec((d, n), lambda i: (0, 0))],
        out_specs=[pl.BlockSpec((tm, n), lambda i: (i, 0)), pl.BlockSpec((tm, d // 2), lambda i: (i, 0))],
        compiler_params=_params(("parallel",), 40),
        name="router",
    )(x, g.reshape(1, d), w)


def norm_matmul(a, col_block, k, g, w, out_dtype, tm, vmem_mb=40, name="norm_matmul"):
    m = a.shape[0]
    n = w.shape[1]
    return pl.pallas_call(
        _norm_mm_kernel,
        out_shape=jax.ShapeDtypeStruct((m, n), out_dtype),
        grid=(m // tm,),
        in_specs=[pl.BlockSpec((tm, k), lambda i: (i, col_block)),
                  pl.BlockSpec((1, k), lambda i: (0, 0)),
                  pl.BlockSpec((k, n), lambda i: (0, 0))],
        out_specs=pl.BlockSpec((tm, n), lambda i: (i, 0)),
        compiler_params=_params(("parallel",), vmem_mb),
        name=name,
    )(a, g.reshape(1, k), w)


def _rope(x, cos_t, sin_a, sin_b):
    return x * cos_t + pltpu.roll(x, 96, 1) * sin_a + pltpu.roll(x, 32, 1) * sin_b


def rope_tables(length, chunks):
    half = MLA_ROPE_DIM // 2
    inv = 1.0 / (ROPE_THETA ** (jnp.arange(0, MLA_ROPE_DIM, 2, dtype=F32) / MLA_ROPE_DIM))
    ang = jnp.arange(length, dtype=F32)[:, None] * inv[None, :]
    cos, sin = jnp.cos(ang), jnp.sin(ang)
    zero = jnp.zeros((length, half), F32)
    c, sa, sb = [], [], []
    for i in range(2):
        on = i < chunks
        c += [cos, cos] if on else [zero, zero]
        sa += [-sin, zero] if on else [zero, zero]
        sb += [zero, sin] if on else [zero, zero]
    return jnp.concatenate(c, 1), jnp.concatenate(sa, 1), jnp.concatenate(sb, 1)


LOG2E = math.log2(math.e)
ATTN_CHAIN_ROWS = 256


def _softmax_pv(s2, v):
    m = jnp.max(s2, axis=-1, keepdims=True)
    p = jnp.exp2(s2 - m)
    l = jnp.sum(p, axis=-1, keepdims=True)
    return jnp.dot(p.astype(BF16), v, preferred_element_type=F32) / l


def _qk(q, k):
    return lax.dot_general(q, k, (((1,), (1,)), ((), ())), preferred_element_type=F32)


def _mla_kernel(q_ref, kn_ref, v_ref, kr_ref, cq_ref, saq_ref, sbq_ref, ck_ref, sak_ref, sbk_ref, o_ref, *, scale):
    q = q_ref[...]
    c = scale * LOG2E
    qn = q[:, :HEAD_DIM].astype(F32) * c
    qr = _rope(q[:, HEAD_DIM:].astype(F32), cq_ref[...], saq_ref[...], sbq_ref[...]) * c
    qf = jnp.concatenate([qn.astype(BF16), qr.astype(BF16)], axis=1)
    kr = _rope(kr_ref[...].astype(F32), ck_ref[...], sak_ref[...], sbk_ref[...])
    kf = jnp.concatenate([kn_ref[...], kr.astype(BF16)], axis=1)
    v = v_ref[...]
    part = ATTN_CHAIN_ROWS
    for r in range(qf.shape[0] // part):
        rows = slice(r * part, (r + 1) * part)
        o_ref[rows, :] = _softmax_pv(_qk(qf[rows], kf), v).astype(o_ref.dtype)


def mla_attention(qf, kv, proj, tabs, batch, length, tq):
    t = batch * length
    nq = length // tq
    scale = (HEAD_DIM + MLA_ROPE_DIM) ** -0.5
    tab_q = pl.BlockSpec((tq, HEAD_DIM), lambda b, h, i: (i, 0))
    tab_k = pl.BlockSpec((length, HEAD_DIM), lambda b, h, i: (0, 0))
    return pl.pallas_call(
        functools.partial(_mla_kernel, scale=scale),
        out_shape=jax.ShapeDtypeStruct((t, BRANCH_WIDTH), BF16),
        grid=(batch, N_HEADS, nq),
        in_specs=[pl.BlockSpec((tq, 2 * HEAD_DIM), lambda b, h, i: (b * nq + i, h)),
                  pl.BlockSpec((length, HEAD_DIM), lambda b, h, i: (b, 2 * h)),
                  pl.BlockSpec((length, HEAD_DIM), lambda b, h, i: (b, 2 * h + 1)),
                  pl.BlockSpec((length, HEAD_DIM), lambda b, h, i: (b, OFF_KR // HEAD_DIM)),
                  tab_q, tab_q, tab_q, tab_k, tab_k, tab_k],
        out_specs=pl.BlockSpec((tq, HEAD_DIM), lambda b, h, i: (b * nq + i, h)),
        compiler_params=_params(("parallel", "parallel", "parallel"), 48),
        name="mla_attention",
    )(qf, kv, kv, proj, *tabs, *tabs)


def _diff_kernel(q_ref, k_ref, v_ref, cq_ref, saq_ref, sbq_ref, ck_ref, sak_ref, sbk_ref, lam_ref, g_ref, o_ref,
                 *, scale, out_scale):
    q = _rope(q_ref[...].astype(F32), cq_ref[...], saq_ref[...], sbq_ref[...])
    q = q * (scale * LOG2E)
    k = _rope(k_ref[...].astype(F32), ck_ref[...], sak_ref[...], sbk_ref[...]).astype(BF16)
    lane = lax.broadcasted_iota(jnp.int32, q.shape, 1)
    q0 = jnp.where(lane < DIFF_QK_DIM, q, 0.0).astype(BF16)
    q1 = jnp.where(lane >= DIFF_QK_DIM, q, 0.0).astype(BF16)
    v = v_ref[...]
    half = ATTN_CHAIN_ROWS
    for r in range(q.shape[0] // half):
        rows = slice(r * half, (r + 1) * half)
        o = _softmax_pv(_qk(q0[rows], k), v) - lam_ref[...] * _softmax_pv(_qk(q1[rows], k), v)
        o_ref[rows, :] = (_norm(o, g_ref[...]) * out_scale).astype(o_ref.dtype)


def diff_attention(proj, tabs, lam_full, subln_g, lambda_init, batch, length, tq):
    t = batch * length
    nq = length // tq
    tab_q = pl.BlockSpec((tq, HEAD_DIM), lambda b, h, i: (i, 0))
    tab_k = pl.BlockSpec((length, HEAD_DIM), lambda b, h, i: (0, 0))
    vec = pl.BlockSpec((1, HEAD_DIM), lambda b, h, i: (0, 0))
    return pl.pallas_call(
        functools.partial(_diff_kernel, scale=DIFF_QK_DIM ** -0.5, out_scale=1.0 - lambda_init),
        out_shape=jax.ShapeDtypeStruct((t, BRANCH_WIDTH), BF16),
        grid=(batch, N_HEADS, nq),
        in_specs=[pl.BlockSpec((tq, HEAD_DIM), lambda b, h, i: (b * nq + i, OFF_QD // HEAD_DIM + h)),
                  pl.BlockSpec((length, HEAD_DIM), lambda b, h, i: (b, OFF_KD // HEAD_DIM + h)),
                  pl.BlockSpec((length, HEAD_DIM), lambda b, h, i: (b, OFF_VD // HEAD_DIM + h)),
                  tab_q, tab_q, tab_q, tab_k, tab_k, tab_k, vec, vec],
        out_specs=pl.BlockSpec((tq, HEAD_DIM), lambda b, h, i: (b * nq + i, h)),
        compiler_params=_params(("parallel", "parallel", "parallel"), 48),
        name="diff_attention",
    )(proj, proj, proj, *tabs, *tabs, jnp.full((1, HEAD_DIM), lam_full, F32), subln_g.reshape(1, HEAD_DIM))


NA_GROUP = 4


def _na_kernel(q_ref, k_ref, v_ref, cb_ref, o_ref, *, rows, kr, win, scale):
    g = pl.program_id(1)
    w_start = jnp.clip(g * NA_GROUP - kr // 2, 0, rows - win)
    base = pl.multiple_of(w_start * GRID_W, GRID_W)
    nk = win * GRID_W
    left = lax.broadcasted_iota(jnp.int32, (GRID_W, 2 * GRID_W), 1) < GRID_W

    slab, row_mask = [], []
    for j in range(NA_GROUP):
        r = g * NA_GROUP + j
        r_start = jnp.clip(r - kr // 2, 0, rows - kr)
        slab_j, mask_j = [], []
        for p in range(win // 2):
            a = w_start + 2 * p
            ok_l = (a >= r_start) & (a < r_start + kr)
            ok_r = (a + 1 >= r_start) & (a + 1 < r_start + kr)
            slab_j.append(jnp.clip(a - r + NA_ROWS_MAX, 0, 2 * NA_ROWS_MAX - 1))
            mask_j.append(jnp.where(left, jnp.where(ok_l, 0.0, MASK_VALUE), jnp.where(ok_r, 0.0, MASK_VALUE)))
        slab.append(slab_j)
        row_mask.append(jnp.concatenate(mask_j, axis=1))

    outs = []
    for h in range(N_HEADS):
        cols = slice(h * HEAD_DIM, (h + 1) * HEAD_DIM)
        qh = (q_ref[:, cols].astype(F32) * (scale * LOG2E)).astype(BF16)
        bias = jnp.concatenate(
            [jnp.concatenate([cb_ref[h, slab[j][p]] for p in range(win // 2)], axis=1) + row_mask[j]
             for j in range(NA_GROUP)], axis=0)
        s2 = _qk(qh, k_ref[pl.ds(base, nk), cols]) + bias
        outs.append(_softmax_pv(s2, v_ref[pl.ds(base, nk), cols]))
    o_ref[...] = jnp.concatenate(outs, axis=1).astype(o_ref.dtype)


def na_bias_slabs(rpb):
    cols = jnp.arange(GRID_W)
    col_start = jnp.clip(cols - NA_COLS // 2, 0, GRID_W - NA_COLS)
    col_in = (cols[None, :] >= col_start[:, None]) & (cols[None, :] < col_start[:, None] + NA_COLS)
    col_idx = jnp.clip(cols[None, :] - cols[:, None], -(NA_COLS - 1), NA_COLS - 1) + NA_COLS - 1
    col_bias = jnp.where(col_in[None, None], rpb[:, :, col_idx].astype(F32) * LOG2E, MASK_VALUE)
    pad = jnp.full((N_HEADS, 1, GRID_W, GRID_W), MASK_VALUE, F32)
    ext = jnp.concatenate([pad, col_bias, pad], axis=1)
    return jnp.concatenate([ext[:, :-1], ext[:, 1:]], axis=-1)


def na_attention(proj, slabs, batch, length):
    t = batch * length
    rows = length // GRID_W
    kr = min(NA_ROWS_MAX, rows)
    win = min(rows, kr + NA_GROUP)
    assert win % 2 == 0 and rows % NA_GROUP == 0
    steps = rows // NA_GROUP
    kv_spec = lambda off: pl.BlockSpec((length, BRANCH_WIDTH), lambda b, r: (b, off // BRANCH_WIDTH))
    return pl.pallas_call(
        functools.partial(_na_kernel, rows=rows, kr=kr, win=win, scale=HEAD_DIM ** -0.5),
        out_shape=jax.ShapeDtypeStruct((t, BRANCH_WIDTH), BF16),
        grid=(batch, steps),
        in_specs=[pl.BlockSpec((NA_GROUP * GRID_W, BRANCH_WIDTH),
                               lambda b, r: (b * steps + r, OFF_QNA // BRANCH_WIDTH)),
                  kv_spec(OFF_KNA), kv_spec(OFF_VNA),
                  pl.BlockSpec((N_HEADS, 2 * NA_ROWS_MAX, GRID_W, 2 * GRID_W), lambda b, r: (0, 0, 0, 0))],
        out_specs=pl.BlockSpec((NA_GROUP * GRID_W, BRANCH_WIDTH), lambda b, r: (b * steps + r, 0)),
        compiler_params=_params(("parallel", "parallel"), 48),
        name="na_attention",
    )(proj, proj, proj, slabs)


HY_TC = 1024
CONV_TC = 512


def _short_conv_kernel(u_ref, w_ref, b_ref, o_ref):
    u = u_ref[...].astype(F32)
    n = u.shape[0]
    row = lax.broadcasted_iota(jnp.int32, u.shape, 0)
    prev = jnp.where(row == 0, 0.0, pltpu.roll(u, 1, 0))
    nxt = jnp.where(row == n - 1, 0.0, pltpu.roll(u, n - 1, 0))
    w = w_ref[...]
    o_ref[...] = (prev * w[0:1] + u * w[1:2] + nxt * w[2:3] + b_ref[...]).astype(o_ref.dtype)


def short_conv(proj, w, bias, batch, length):
    t = batch * length
    width = 3 * HYENA_WIDTH
    return pl.pallas_call(
        _short_conv_kernel,
        out_shape=jax.ShapeDtypeStruct((t, width), BF16),
        grid=(batch, width // CONV_TC),
        in_specs=[pl.BlockSpec((length, CONV_TC), lambda b, c: (b, c)),
                  pl.BlockSpec((3, CONV_TC), lambda b, c: (0, c)),
                  pl.BlockSpec((1, CONV_TC), lambda b, c: (0, c))],
        out_specs=pl.BlockSpec((length, CONV_TC), lambda b, c: (b, c)),
        compiler_params=_params(("parallel", "parallel"), 48),
        name="short_conv",
    )(proj, w, bias.reshape(1, width))


def dft_matrices(length):
    n = 2 * length
    kb = min(512, length)
    nkb = length // kb
    k = jnp.arange(length, dtype=jnp.int32)[:, None]
    s = jnp.arange(length, dtype=jnp.int32)[None, :]
    ang = ((k * s) % n).astype(F32) * (2.0 * math.pi / n)
    c, sn = jnp.cos(ang), jnp.sin(ang)
    alt_s = jnp.where(s % 2 == 0, 1.0, -1.0).astype(F32)
    alt_t = jnp.where(k % 2 == 0, 1.0, -1.0).astype(F32)
    f_re = c.astype(BF16)
    f_im = jnp.where(k == 0, alt_s, -sn).astype(BF16)
    g_c = jnp.where(s == 0, 1.0 / n, c * (2.0 / n)).astype(BF16)
    g_s = jnp.where(s == 0, alt_t / n, -sn * (2.0 / n)).astype(BF16)
    blk = lambda i: slice(i * kb, (i + 1) * kb)
    f_fwd = jnp.concatenate([m[blk(i)] for i in range(nkb) for m in (f_re, f_im)], axis=0)
    g_inv = jnp.concatenate([m[:, blk(i)] for i in range(nkb) for m in (g_c, g_s)], axis=1)
    return f_fwd, g_inv


def hyena_filter_spectrum(length, w1, b1, w2, b2, w3, freq, decay, f_fwd):
    hp = lax.Precision.HIGHEST
    pos = jnp.arange(length, dtype=F32)
    tt = (pos / max(length - 1, 1))[:, None]
    bands = jnp.linspace(1e-4, HYENA_BANDS - 1, HYENA_BANDS, dtype=F32)
    ang = (2.0 * math.pi / length) * pos[:, None] * bands[None, :]
    feats = jnp.concatenate([tt, jnp.cos(ang), jnp.sin(ang)], axis=-1)
    h = jnp.sin(freq[0] * (jnp.dot(feats, w1, precision=hp) + b1))
    h = jnp.sin(freq[1] * (jnp.dot(h, w2, precision=hp) + b2))
    h = jnp.dot(h, w3, precision=hp).reshape(length, 2, 2, HYENA_WIDTH)
    h = h * jnp.exp(-tt[:, :, None, None] * jnp.abs(decay))
    width = 2 * HYENA_WIDTH
    fwd = h[:, :, 0].reshape(length, width)
    bwd = jnp.where(pos[:, None] == 0, 0.0, h[:, :, 1].reshape(length, width))
    inv_l1 = 1.0 / (jnp.sum(jnp.abs(fwd), axis=0) + jnp.sum(jnp.abs(bwd), axis=0))
    n = 2 * length
    kb = min(512, length)
    spec = matmul(f_fwd, jnp.concatenate([fwd, bwd], axis=1).astype(BF16), 2 * width, F32,
                  tm=min(1024, n), tn=512, name="filter_dft")
    re_rows = lambda i: slice(2 * i * kb, (2 * i + 1) * kb)
    im_rows = lambda i: slice((2 * i + 1) * kb, (2 * i + 2) * kb)
    nkb = length // kb
    k_re = jnp.concatenate([spec[re_rows(i), :width] + spec[re_rows(i), width:] for i in range(nkb)], axis=0) * inv_l1
    k_im = jnp.concatenate([spec[im_rows(i), :width] - spec[im_rows(i), width:] for i in range(nkb)], axis=0) * inv_l1
    nyquist = (spec[kb, :width] + spec[kb, width:]) * inv_l1
    first = jnp.arange(length)[:, None] == 0
    return k_re, jnp.where(first, 0.0, k_im), jnp.where(first, nyquist[None, :], k_re)


def _dft_fwd_kernel(f_ref, z_ref, a_ref, b_ref, a2_ref, o_ref):
    res = jnp.dot(f_ref[...], z_ref[...], preferred_element_type=F32)
    kb = res.shape[0] // 2
    z_re, z_im = res[:kb], res[kb:]
    b = b_ref[...]
    o_ref[:kb, :] = (z_re * a_ref[...] - z_im * b).astype(o_ref.dtype)
    o_ref[kb:, :] = (z_re * b + z_im * a2_ref[...]).astype(o_ref.dtype)


def dft_forward(f_fwd, z, z_col, coefs, order, batch, length):
    n = 2 * length
    kb = min(512, length)
    nkb = length // kb
    nc = HYENA_WIDTH // HY_TC
    coef = pl.BlockSpec((kb, HY_TC), lambda b, c, k: (k, order * nc + c))
    return pl.pallas_call(
        _dft_fwd_kernel,
        out_shape=jax.ShapeDtypeStruct((batch * n, HYENA_WIDTH), BF16),
        grid=(batch, nc, nkb),
        in_specs=[pl.BlockSpec((2 * kb, length), lambda b, c, k: (k, 0)),
                  pl.BlockSpec((length, HY_TC), lambda b, c, k: (b, z_col + c)),
                  coef, coef, coef],
        out_specs=pl.BlockSpec((2 * kb, HY_TC), lambda b, c, k: (b * nkb + k, c)),
        compiler_params=_params(("parallel", "parallel", "parallel"), 56),
        name="dft_forward",
    )(f_fwd, z, *coefs)


def _dft_inv_kernel(g_ref, y_ref, x_ref, z_ref, bias_ref, o_ref):
    y = jnp.dot(g_ref[...], y_ref[...], preferred_element_type=F32)
    o_ref[...] = (x_ref[...].astype(F32) * (y + bias_ref[...] * z_ref[...].astype(F32))).astype(o_ref.dtype)


def dft_inverse(g_inv, spec, gate, gate_col, z, z_col, bias, batch, length):
    n = 2 * length
    tt = min(1024, length)
    nt = length // tt
    nc = HYENA_WIDTH // HY_TC
    return pl.pallas_call(
        _dft_inv_kernel,
        out_shape=jax.ShapeDtypeStruct((batch * length, HYENA_WIDTH), BF16),
        grid=(batch, nc, nt),
        in_specs=[pl.BlockSpec((tt, n), lambda b, c, i: (i, 0)),
                  pl.BlockSpec((n, HY_TC), lambda b, c, i: (b, c)),
                  pl.BlockSpec((tt, HY_TC), lambda b, c, i: (b * nt + i, gate_col + c)),
                  pl.BlockSpec((tt, HY_TC), lambda b, c, i: (b * nt + i, z_col + c)),
                  pl.BlockSpec((1, HY_TC), lambda b, c, i: (0, c))],
        out_specs=pl.BlockSpec((tt, HY_TC), lambda b, c, i: (b * nt + i, c)),
        compiler_params=_params(("parallel", "parallel", "parallel"), 56),
        name="dft_inverse",
    )(g_inv, spec, gate, z, bias.reshape(1, HYENA_WIDTH))


def hyena_mixer(proj, conv_w, conv_b, coefs, bias, f_fwd, g_inv, batch, length):
    nc = HYENA_WIDTH // HY_TC
    cv = short_conv(proj, conv_w, conv_b, batch, length)
    spec = dft_forward(f_fwd, cv, 0, coefs, 0, batch, length)
    z1 = dft_inverse(g_inv, spec, cv, nc, cv, 0, bias[0], batch, length)
    spec = dft_forward(f_fwd, z1, 0, coefs, 1, batch, length)
    return dft_inverse(g_inv, spec, cv, 2 * nc, z1, 0, bias[1], batch, length)


def _merge_kernel(h_ref, g0, g1, g2, g3, y0, y1, y2, y3, wb_ref, o_ref):
    h = h_ref[...]
    acc = None
    for i, (g_ref, y_ref) in enumerate(((g0, y0), (g1, y1), (g2, y2), (g3, y3))):
        gate = jax.nn.sigmoid(jnp.dot(h, g_ref[...], preferred_element_type=F32))
        term = gate * jnp.dot(y_ref[...], wb_ref[i], preferred_element_type=F32)
        acc = term if acc is None else acc + term
    o_ref[...] = acc.astype(o_ref.dtype)


def gated_merge(h, w_gate, ys, w_branch, tm=512, tn=256):
    t, d = h.shape
    gate_spec = lambda i: pl.BlockSpec((d, tn), lambda n, m: (0, (i * d) // tn + n))
    y_spec = pl.BlockSpec((tm, BRANCH_WIDTH), lambda n, m: (m, 0))
    return pl.pallas_call(
        _merge_kernel,
        out_shape=jax.ShapeDtypeStruct((t, d), BF16),
        grid=(d // tn, t // tm),
        in_specs=[pl.BlockSpec((tm, d), lambda n, m: (m, 0))] + [gate_spec(i) for i in range(4)] + [y_spec] * 4
                 + [pl.BlockSpec((4, BRANCH_WIDTH, tn), lambda n, m: (0, 0, n))],
        out_specs=pl.BlockSpec((tm, tn), lambda n, m: (m, n)),
        compiler_params=_params(("parallel", "parallel"), 52),
        name="gated_merge",
    )(h, w_gate, w_gate, w_gate, w_gate, *ys, w_branch)


def moe_routing(logits, rg_b, re_b, tm, n_tiles):
    t = logits.shape[0]
    g_prob = jax.nn.softmax(logits[:, :N_GROUPS] + rg_b, axis=-1)
    g_idx = jnp.argmax(g_prob, axis=-1)
    g_val = jnp.max(g_prob, axis=-1)
    e_logit = (logits[:, N_GROUPS:N_GROUPS + N_EXPERTS] + re_b).reshape(t, N_GROUPS, EXPERTS_PER_GROUP)
    e_sel = jnp.take_along_axis(e_logit, g_idx[:, None, None], axis=1)[:, 0]
    top_val, top_idx = lax.top_k(jax.nn.softmax(e_sel, axis=-1), 2)
    top_val = top_val / jnp.sum(top_val, axis=-1, keepdims=True)
    weight = (g_val[:, None] * top_val).T.reshape(-1)
    expert = (g_idx[:, None] * EXPERTS_PER_GROUP + top_idx).T.reshape(-1).astype(jnp.int32)

    order = jnp.argsort(expert, stable=True).astype(jnp.int32)
    counts = jnp.sum(expert[None, :] == jnp.arange(N_EXPERTS, dtype=jnp.int32)[:, None], axis=1, dtype=jnp.int32)
    padded = ((counts + tm - 1) // tm) * tm
    pad_end = jnp.cumsum(padded)
    pad_start = pad_end - padded
    start = jnp.cumsum(counts) - counts
    tile_start = jnp.arange(n_tiles, dtype=jnp.int32) * tm
    tile_expert = jnp.minimum(jnp.sum(pad_end[None, :] <= tile_start[:, None], axis=1, dtype=jnp.int32), N_EXPERTS - 1)
    n_used = (pad_end[-1] // tm).astype(jnp.int32).reshape(1)
    in_tile = jnp.arange(tm, dtype=jnp.int32)[None, :]
    off = (tile_start - pad_start[tile_expert])[:, None] + in_tile
    valid = (off < counts[tile_expert][:, None]) & (tile_start < pad_end[-1])[:, None]
    src = order[jnp.clip(start[tile_expert][:, None] + off, 0, 2 * t - 1)]
    row_tok = jnp.where(valid, src % t, 0).astype(jnp.int32)
    row_dst = jnp.where(valid, src, 2 * t + in_tile).astype(jnp.int32)
    row_w = jnp.where(valid, weight[src], 0.0).astype(F32)
    return tile_expert, n_used, row_tok, row_dst, row_w


MOE_UNROLL = 32


def _moe_kernel(te_ref, nu_ref, tok_ref, tok_next_ref, dst_ref, w_ref, x_hbm, wg_ref, wu_ref, wd_ref, o_hbm,
                xbuf, obuf, gsem, ssem):
    tm = xbuf.shape[1]
    i = pl.program_id(0)
    n_used = nu_ref[0]
    slot = i % 2

    def issue_gather(idx_ref, s):
        @pl.loop(0, tm // MOE_UNROLL)
        def _(c):
            for j in range(MOE_UNROLL):
                r = c * MOE_UNROLL + j
                pltpu.make_async_copy(x_hbm.at[pl.ds(idx_ref[0, 0, r], 1)], xbuf.at[s, pl.ds(r, 1)],
                                      gsem.at[s]).start()

    def wait_gather(s):
        pltpu.make_async_copy(x_hbm.at[pl.ds(0, tm)], xbuf.at[s], gsem.at[s]).wait()

    def wait_scatter(s):
        pltpu.make_async_copy(obuf.at[s], o_hbm.at[pl.ds(0, tm)], ssem.at[s]).wait()

    @pl.when(i < n_used)
    def _():
        @pl.when(i == 0)
        def _():
            issue_gather(tok_ref, slot)
            obuf[1] = jnp.zeros(obuf.shape[1:], obuf.dtype)
            init = pltpu.make_async_copy(obuf.at[1], o_hbm.at[pl.ds(o_hbm.shape[0] - tm, tm)], ssem.at[1])
            init.start()
            init.wait()

        wait_gather(slot)
        xn = unpack_halves(xbuf[slot]).astype(BF16)
        issue_gather(tok_next_ref, 1 - slot)
        a = jnp.dot(xn, wg_ref[0], preferred_element_type=F32)
        u = jnp.dot(xn, wu_ref[0], preferred_element_type=F32)
        hid = (a * jax.nn.sigmoid(a) * u * w_ref[...]).astype(BF16)
        res = pack_halves(jnp.dot(hid, wd_ref[0], preferred_element_type=F32))

        @pl.when(i >= 2)
        def _():
            wait_scatter(slot)

        obuf[slot] = res

        @pl.loop(0, tm // MOE_UNROLL)
        def _(c):
            for j in range(MOE_UNROLL):
                r = c * MOE_UNROLL + j
                pltpu.make_async_copy(obuf.at[slot, pl.ds(r, 1)], o_hbm.at[pl.ds(dst_ref[0, 0, r], 1)],
                                      ssem.at[slot]).start()

        @pl.when(i == n_used - 1)
        def _():
            wait_gather(1 - slot)
            wait_scatter(slot)

            @pl.when(i >= 1)
            def _():
                wait_scatter(1 - slot)


def moe_experts(xp, schedule, w_gate, w_up, w_down, layer, tm):
    t, dp = xp.shape
    d = 2 * dp
    tile_expert, n_used, row_tok, row_dst, row_w = schedule
    n_tiles = tile_expert.shape[0]
    w_idx = lambda i, te, nu: (layer * N_EXPERTS + te[i], 0, 0)
    idx_spec = lambda off: pl.BlockSpec((1, 1, tm), lambda i, te, nu: (jnp.minimum(i + off, n_tiles - 1), 0, 0),
                                        memory_space=pltpu.SMEM)
    grid_spec = pltpu.PrefetchScalarGridSpec(
        num_scalar_prefetch=2,
        grid=(n_tiles,),
        in_specs=[idx_spec(0), idx_spec(1), idx_spec(0),
                  pl.BlockSpec((tm, 1), lambda i, te, nu: (i, 0)),
                  pl.BlockSpec(memory_space=pl.ANY),
                  pl.BlockSpec((1, d, EXPERT_FF), w_idx),
                  pl.BlockSpec((1, d, EXPERT_FF), w_idx),
                  pl.BlockSpec((1, EXPERT_FF, d), w_idx)],
        out_specs=pl.BlockSpec(memory_space=pl.ANY),
        scratch_shapes=[pltpu.VMEM((2, tm, dp), jnp.uint32), pltpu.VMEM((2, tm, dp), jnp.uint32),
                        pltpu.SemaphoreType.DMA((2,)), pltpu.SemaphoreType.DMA((2,))],
    )
    tok3 = row_tok.reshape(n_tiles, 1, tm)
    return pl.pallas_call(
        _moe_kernel,
        out_shape=jax.ShapeDtypeStruct((2 * t + tm, dp), jnp.uint32),
        grid_spec=grid_spec,
        compiler_params=_params(("arbitrary",), 48),
        name="moe_experts",
    )(tile_expert, n_used, tok3, tok3, row_dst.reshape(n_tiles, 1, tm),
      row_w.reshape(n_tiles * tm, 1), xp, w_gate, w_up, w_down)


def split_w_in(w):
    d = w.shape[0]
    cols = lambda a, b: w[:, a:b].astype(BF16)
    z = lambda n: jnp.zeros((d, n), BF16)
    w_mla = jnp.concatenate([cols(IN_CQ, IN_CKV), cols(IN_KR, IN_HY), z(64), z(128), cols(IN_CKV, IN_KR)], axis=1)
    return cols(IN_QKV, IN_GATE), cols(IN_HY, IN_QKV), w_mla, cols(IN_GATE, IN_END)


def pack_w_uq(w):
    r = w.shape[0]
    w = w.reshape(r, N_HEADS, HEAD_DIM + MLA_ROPE_DIM)
    w = jnp.concatenate([w, jnp.zeros((r, N_HEADS, HEAD_DIM - MLA_ROPE_DIM), w.dtype)], axis=-1)
    return w.reshape(r, N_HEADS * 2 * HEAD_DIM).astype(BF16)


def pack_router(rg_w, re_w):
    d = rg_w.shape[0]
    pad = jnp.zeros((d, HEAD_DIM - N_GROUPS - N_EXPERTS), rg_w.dtype)
    return jnp.concatenate([rg_w, re_w, pad], axis=1).astype(BF16)


def kernel(x, norm_mix_g, w_in, mla_q_norm_g, mla_kv_norm_g, mla_w_uq, mla_w_ukv, hyena_conv_w, hyena_conv_b, hyena_ffn_w1, hyena_ffn_b1, hyena_ffn_w2, hyena_ffn_b2, hyena_ffn_w3, hyena_sin_freq, hyena_decay, hyena_bias, diff_lambda, diff_subln_g, na_rpb, w_branch, w_out, norm_ffn_g, router_group_w, router_group_b, router_expert_w, router_expert_b, moe_w_gate, moe_w_up, moe_w_down, norm_final_g):
    batch, length, d = x.shape
    t = batch * length
    assert d == D_MODEL and length % GRID_W == 0 and t % 1024 == 0
    tq_mla, tq_diff = min(1024, length), min(512, length)
    n_tiles = 2 * t // MOE_TM + N_EXPERTS

    rope_mla = rope_tables(length, 1)
    rope_diff = rope_tables(length, 2)
    f_fwd, g_inv = dft_matrices(length)
    moe_wg = moe_w_gate.astype(BF16).reshape(DEPTH * N_EXPERTS, d, EXPERT_FF)
    moe_wu = moe_w_up.astype(BF16).reshape(DEPTH * N_EXPERTS, d, EXPERT_FF)
    moe_wd = moe_w_down.astype(BF16).reshape(DEPTH * N_EXPERTS, EXPERT_FF, d)

    x2 = x.reshape(t, d)
    h = rms_norm(x2, norm_mix_g[0], BF16)
    out = None
    for l in range(DEPTH):
        w_qkv, w_hy, w_mla, w_gate = split_w_in(w_in[l])
        proj_qkv = matmul(h, w_qkv, N_QKV, BF16, tm=1024, tn=512, name="in_proj_qkv")
        proj_hy = matmul(h, w_hy, N_HY, BF16, tm=1024, tn=512, name="in_proj_hy")
        proj_mla = matmul(h, w_mla, N_MLA, BF16, tm=1024, tn=512, name="in_proj_mla")

        qf = norm_matmul(proj_mla, OFF_CQ // MLA_Q_RANK, MLA_Q_RANK, mla_q_norm_g[l], pack_w_uq(mla_w_uq[l]), BF16,
                         tm=512, name="mla_q_up")
        kv = norm_matmul(proj_mla, OFF_CKV // MLA_KV_RANK, MLA_KV_RANK, mla_kv_norm_g[l], mla_w_ukv[l].astype(BF16),
                         BF16, tm=512, name="mla_kv_up")
        y_a = mla_attention(qf, kv, proj_mla, rope_mla, batch, length, tq_mla)

        coefs = hyena_filter_spectrum(length, hyena_ffn_w1[l], hyena_ffn_b1[l], hyena_ffn_w2[l], hyena_ffn_b2[l],
                                      hyena_ffn_w3[l], hyena_sin_freq[l], hyena_decay[l], f_fwd)
        y_b = hyena_mixer(proj_hy, hyena_conv_w[l], hyena_conv_b[l], coefs, hyena_bias[l], f_fwd, g_inv, batch, length)

        lambda_init = 0.8 - 0.6 * math.exp(-0.3 * l)
        lam = diff_lambda[l].astype(F32)
        lam_full = jnp.exp(jnp.sum(lam[0] * lam[1])) - jnp.exp(jnp.sum(lam[2] * lam[3])) + lambda_init
        y_c = diff_attention(proj_qkv, rope_diff, lam_full, diff_subln_g[l], lambda_init, batch, length, tq_diff)

        y_d = na_attention(proj_qkv, na_bias_slabs(na_rpb[l]), batch, length)

        merged = gated_merge(h, w_gate, (y_a, y_b, y_c, y_d), w_branch[l].astype(BF16))
        x_mid = matmul(merged, w_out[l].astype(BF16), d, F32, tm=1024, tn=512, res=x2, name="out_proj")

        logits, xp = router(x_mid, norm_ffn_g[l], pack_router(router_group_w[l], router_expert_w[l]))
        schedule = moe_routing(logits, router_group_b[l], router_expert_b[l], MOE_TM, n_tiles)
        out2 = moe_experts(xp, schedule, moe_wg, moe_wu, moe_wd, l, MOE_TM)
        if l + 1 < DEPTH:
            x2, h = combine_rms_norm(x_mid, out2, norm_mix_g[l + 1], BF16, emit_x=True)
        else:
            out = combine_rms_norm(x_mid, out2, norm_final_g, F32, emit_x=False)[0]
    return out.reshape(batch, length, d)
```

```python
import functools
import math

import jax
import jax.numpy as jnp
from jax import lax
from jax.experimental import pallas as pl
from jax.experimental.pallas import tpu as pltpu

F32 = jnp.float32
BF16 = jnp.bfloat16

D_MODEL = 4096
DEPTH = 2
NORM_EPS = 1e-6
ROPE_THETA = 10000.0
BRANCH_WIDTH = D_MODEL // 4
HEAD_DIM = 128
N_HEADS = BRANCH_WIDTH // HEAD_DIM
MLA_ROPE_DIM = 64
MLA_Q_RANK = (3 * D_MODEL) // 16
MLA_KV_RANK = 512
HYENA_WIDTH = BRANCH_WIDTH
HYENA_BANDS = 16
HYENA_DECAY_TARGET = 1e-2
DIFF_QK_DIM = 64
GRID_W = 64
NA_ROWS_MAX = 8
NA_COLS = 16
N_GROUPS = 4
EXPERTS_PER_GROUP = 8
N_EXPERTS = N_GROUPS * EXPERTS_PER_GROUP
EXPERT_FF = D_MODEL // 8
MASK_VALUE = -1e30

IN_CQ, IN_CKV, IN_KR, IN_HY, IN_QKV, IN_GATE, IN_END = 0, 768, 1280, 1344, 4416, 10560, 26944
OFF_QD, OFF_KD, OFF_VD = 0, 1024, 2048
OFF_QNA, OFF_KNA, OFF_VNA = 3072, 4096, 5120
N_QKV = 6144
N_HY = 3072
OFF_CQ, OFF_KR, OFF_CKV = 0, 768, 1024
N_MLA = 1536

MOE_TM = 256


def _params(semantics, vmem_mb):
    return pltpu.CompilerParams(dimension_semantics=semantics, vmem_limit_bytes=vmem_mb << 20)


def _norm(x, g):
    return x * lax.rsqrt(jnp.mean(x * x, axis=-1, keepdims=True) + NORM_EPS) * g


def _rms_kernel(x_ref, g_ref, h_ref):
    h_ref[...] = _norm(x_ref[...], g_ref[...]).astype(h_ref.dtype)


def rms_norm(x, g, out_dtype, tm=256):
    t, d = x.shape
    return pl.pallas_call(
        _rms_kernel,
        out_shape=jax.ShapeDtypeStruct((t, d), out_dtype),
        grid=(t // tm,),
        in_specs=[pl.BlockSpec((tm, d), lambda i: (i, 0)), pl.BlockSpec((1, d), lambda i: (0, 0))],
        out_specs=pl.BlockSpec((tm, d), lambda i: (i, 0)),
        compiler_params=_params(("parallel",), 40),
        name="rms_norm",
    )(x, g.reshape(1, d))


def _bf16_bits(a):
    return lax.bitcast_convert_type(a.astype(BF16).astype(F32), jnp.uint32)


def pack_halves(a):
    n = a.shape[1] // 2
    return (_bf16_bits(a[:, :n]) >> 16) | _bf16_bits(a[:, n:])


def unpack_halves(w):
    lo = lax.bitcast_convert_type(w << 16, F32)
    hi = lax.bitcast_convert_type(w & jnp.uint32(0xFFFF0000), F32)
    return jnp.concatenate([lo, hi], axis=1)


def _combine_rms_kernel(x_ref, a_ref, b_ref, g_ref, *out_refs):
    x = x_ref[...] + unpack_halves(a_ref[...]) + unpack_halves(b_ref[...])
    if len(out_refs) == 2:
        out_refs[0][...] = x
    out_refs[-1][...] = _norm(x, g_ref[...]).astype(out_refs[-1].dtype)


def combine_rms_norm(x, out2, g, out_dtype, emit_x, tm=256):
    t, d = x.shape
    nb = t // tm
    row = pl.BlockSpec((tm, d), lambda i: (i, 0))
    shapes = [jax.ShapeDtypeStruct((t, d), out_dtype)]
    if emit_x:
        shapes = [jax.ShapeDtypeStruct((t, d), F32)] + shapes
    return pl.pallas_call(
        _combine_rms_kernel,
        out_shape=shapes,
        grid=(nb,),
        in_specs=[row, pl.BlockSpec((tm, d // 2), lambda i: (i, 0)), pl.BlockSpec((tm, d // 2), lambda i: (i + nb, 0)),
                  pl.BlockSpec((1, d), lambda i: (0, 0))],
        out_specs=[row] * len(shapes),
        compiler_params=_params(("parallel",), 56),
        name="combine_rms_norm",
    )(x, out2, out2, g.reshape(1, d))


def _mm_kernel(a_ref, w_ref, o_ref):
    o_ref[...] = jnp.dot(a_ref[...], w_ref[...], preferred_element_type=F32).astype(o_ref.dtype)


def _mm_res_kernel(a_ref, w_ref, r_ref, o_ref):
    o_ref[...] = r_ref[...] + jnp.dot(a_ref[...], w_ref[...], preferred_element_type=F32)


def matmul(a, w, n_out, out_dtype, tm, tn, res=None, vmem_mb=48, name="matmul"):
    m, k = a.shape
    in_specs = [pl.BlockSpec((tm, k), lambda i, j: (i, 0)), pl.BlockSpec((k, tn), lambda i, j: (0, j))]
    args = [a, w]
    body = _mm_kernel
    if res is not None:
        in_specs.append(pl.BlockSpec((tm, tn), lambda i, j: (i, j)))
        args.append(res)
        body = _mm_res_kernel
    return pl.pallas_call(
        body,
        out_shape=jax.ShapeDtypeStruct((m, n_out), out_dtype),
        grid=(m // tm, n_out // tn),
        in_specs=in_specs,
        out_specs=pl.BlockSpec((tm, tn), lambda i, j: (i, j)),
        compiler_params=_params(("parallel", "parallel"), vmem_mb),
        name=name,
    )(*args)


def _norm_mm_kernel(a_ref, g_ref, w_ref, o_ref):
    an = _norm(a_ref[...].astype(F32), g_ref[...]).astype(BF16)
    o_ref[...] = jnp.dot(an, w_ref[...], preferred_element_type=F32).astype(o_ref.dtype)


def _router_kernel(x_ref, g_ref, w_ref, logits_ref, xp_ref):
    xn = _norm(x_ref[...], g_ref[...])
    logits_ref[...] = jnp.dot(xn.astype(BF16), w_ref[...], preferred_element_type=F32)
    xp_ref[...] = pack_halves(xn)


def router(x, g, w, tm=256):
    t, d = x.shape
    n = w.shape[1]
    return pl.pallas_call(
        _router_kernel,
        out_shape=[jax.ShapeDtypeStruct((t, n), F32), jax.ShapeDtypeStruct((t, d // 2), jnp.uint32)],
        grid=(t // tm,),
        in_specs=[pl.BlockSpec((tm, d), lambda i: (i, 0)),
                  pl.BlockSpec((1, d), lambda i: (0, 0)),
                  pl.BlockSpec((d, n), lambda i: (0, 0))],
        out_specs=[pl.BlockSpec((tm, n), lambda i: (i, 0)), pl.BlockSpec((tm, d // 2), lambda i: (i, 0))],
        compiler_params=_params(("parallel",), 40),
        name="router",
    )(x, g.reshape(1, d), w)


def norm_matmul(a, col_block, k, g, w, out_dtype, tm, vmem_mb=40, name="norm_matmul"):
    m = a.shape[0]
    n = w.shape[1]
    return pl.pallas_call(
        _norm_mm_kernel,
        out_shape=jax.ShapeDtypeStruct((m, n), out_dtype),
        grid=(m // tm,),
        in_specs=[pl.BlockSpec((tm, k), lambda i: (i, col_block)),
                  pl.BlockSpec((1, k), lambda i: (0, 0)),
                  pl.BlockSpec((k, n), lambda i: (0, 0))],
        out_specs=pl.BlockSpec((tm, n), lambda i: (i, 0)),
        compiler_params=_params(("parallel",), vmem_mb),
        name=name,
    )(a, g.reshape(1, k), w)


def _rope(x, cos_t, sin_a, sin_b):
    return x * cos_t + pltpu.roll(x, 96, 1) * sin_a + pltpu.roll(x, 32, 1) * sin_b


def rope_tables(length, chunks):
    half = MLA_ROPE_DIM // 2
    inv = 1.0 / (ROPE_THETA ** (jnp.arange(0, MLA_ROPE_DIM, 2, dtype=F32) / MLA_ROPE_DIM))
    ang = jnp.arange(length, dtype=F32)[:, None] * inv[None, :]
    cos, sin = jnp.cos(ang), jnp.sin(ang)
    zero = jnp.zeros((length, half), F32)
    c, sa, sb = [], [], []
    for i in range(2):
        on = i < chunks
        c += [cos, cos] if on else [zero, zero]
        sa += [-sin, zero] if on else [zero, zero]
        sb += [zero, sin] if on else [zero, zero]
    return jnp.concatenate(c, 1), jnp.concatenate(sa, 1), jnp.concatenate(sb, 1)


LOG2E = math.log2(math.e)
ATTN_CHAIN_ROWS = 256


def _softmax_pv(s2, v):
    m = jnp.max(s2, axis=-1, keepdims=True)
    p = jnp.exp2(s2 - m)
    l = jnp.sum(p, axis=-1, keepdims=True)
    return jnp.dot(p.astype(BF16), v, preferred_element_type=F32) / l


def _qk(q, k):
    return lax.dot_general(q, k, (((1,), (1,)), ((), ())), preferred_element_type=F32)


def _mla_kernel(q_ref, kn_ref, v_ref, kr_ref, cq_ref, saq_ref, sbq_ref, ck_ref, sak_ref, sbk_ref, o_ref, *, scale):
    q = q_ref[...]
    c = scale * LOG2E
    qn = q[:, :HEAD_DIM].astype(F32) * c
    qr = _rope(q[:, HEAD_DIM:].astype(F32), cq_ref[...], saq_ref[...], sbq_ref[...]) * c
    qf = jnp.concatenate([qn.astype(BF16), qr.astype(BF16)], axis=1)
    kr = _rope(kr_ref[...].astype(F32), ck_ref[...], sak_ref[...], sbk_ref[...])
    kf = jnp.concatenate([kn_ref[...], kr.astype(BF16)], axis=1)
    v = v_ref[...]
    part = ATTN_CHAIN_ROWS
    for r in range(qf.shape[0] // part):
        rows = slice(r * part, (r + 1) * part)
        o_ref[rows, :] = _softmax_pv(_qk(qf[rows], kf), v).astype(o_ref.dtype)


def mla_attention(qf, kv, proj, tabs, batch, length, tq):
    t = batch * length
    nq = length // tq
    scale = (HEAD_DIM + MLA_ROPE_DIM) ** -0.5
    tab_q = pl.BlockSpec((tq, HEAD_DIM), lambda b, h, i: (i, 0))
    tab_k = pl.BlockSpec((length, HEAD_DIM), lambda b, h, i: (0, 0))
    return pl.pallas_call(
        functools.partial(_mla_kernel, scale=scale),
        out_shape=jax.ShapeDtypeStruct((t, BRANCH_WIDTH), BF16),
        grid=(batch, N_HEADS, nq),
        in_specs=[pl.BlockSpec((tq, 2 * HEAD_DIM), lambda b, h, i: (b * nq + i, h)),
                  pl.BlockSpec((length, HEAD_DIM), lambda b, h, i: (b, 2 * h)),
                  pl.BlockSpec((length, HEAD_DIM), lambda b, h, i: (b, 2 * h + 1)),
                  pl.BlockSpec((length, HEAD_DIM), lambda b, h, i: (b, OFF_KR // HEAD_DIM)),
                  tab_q, tab_q, tab_q, tab_k, tab_k, tab_k],
        out_specs=pl.BlockSpec((tq, HEAD_DIM), lambda b, h, i: (b * nq + i, h)),
        compiler_params=_params(("parallel", "parallel", "parallel"), 48),
        name="mla_attention",
    )(qf, kv, kv, proj, *tabs, *tabs)


def _diff_kernel(q_ref, k_ref, v_ref, cq_ref, saq_ref, sbq_ref, ck_ref, sak_ref, sbk_ref, lam_ref, g_ref, o_ref,
                 *, scale, out_scale):
    q = _rope(q_ref[...].astype(F32), cq_ref[...], saq_ref[...], sbq_ref[...])
    q = q * (scale * LOG2E)
    k = _rope(k_ref[...].astype(F32), ck_ref[...], sak_ref[...], sbk_ref[...]).astype(BF16)
    lane = lax.broadcasted_iota(jnp.int32, q.shape, 1)
    q0 = jnp.where(lane < DIFF_QK_DIM, q, 0.0).astype(BF16)
    q1 = jnp.where(lane >= DIFF_QK_DIM, q, 0.0).astype(BF16)
    v = v_ref[...]
    half = ATTN_CHAIN_ROWS
    for r in range(q.shape[0] // half):
        rows = slice(r * half, (r + 1) * half)
        o = _softmax_pv(_qk(q0[rows], k), v) - lam_ref[...] * _softmax_pv(_qk(q1[rows], k), v)
        o_ref[rows, :] = (_norm(o, g_ref[...]) * out_scale).astype(o_ref.dtype)


def diff_attention(proj, tabs, lam_full, subln_g, lambda_init, batch, length, tq):
    t = batch * length
    nq = length // tq
    tab_q = pl.BlockSpec((tq, HEAD_DIM), lambda b, h, i: (i, 0))
    tab_k = pl.BlockSpec((length, HEAD_DIM), lambda b, h, i: (0, 0))
    vec = pl.BlockSpec((1, HEAD_DIM), lambda b, h, i: (0, 0))
    return pl.pallas_call(
        functools.partial(_diff_kernel, scale=DIFF_QK_DIM ** -0.5, out_scale=1.0 - lambda_init),
        out_shape=jax.ShapeDtypeStruct((t, BRANCH_WIDTH), BF16),
        grid=(batch, N_HEADS, nq),
        in_specs=[pl.BlockSpec((tq, HEAD_DIM), lambda b, h, i: (b * nq + i, OFF_QD // HEAD_DIM + h)),
                  pl.BlockSpec((length, HEAD_DIM), lambda b, h, i: (b, OFF_KD // HEAD_DIM + h)),
                  pl.BlockSpec((length, HEAD_DIM), lambda b, h, i: (b, OFF_VD // HEAD_DIM + h)),
                  tab_q, tab_q, tab_q, tab_k, tab_k, tab_k, vec, vec],
        out_specs=pl.BlockSpec((tq, HEAD_DIM), lambda b, h, i: (b * nq + i, h)),
        compiler_params=_params(("parallel", "parallel", "parallel"), 48),
        name="diff_attention",
    )(proj, proj, proj, *tabs, *tabs, jnp.full((1, HEAD_DIM), lam_full, F32), subln_g.reshape(1, HEAD_DIM))


NA_GROUP = 4


def _na_kernel(q_ref, k_ref, v_ref, cb_ref, o_ref, *, rows, kr, win, scale):
    g = pl.program_id(1)
    w_start = jnp.clip(g * NA_GROUP - kr // 2, 0, rows - win)
    base = pl.multiple_of(w_start * GRID_W, GRID_W)
    nk = win * GRID_W
    left = lax.broadcasted_iota(jnp.int32, (GRID_W, 2 * GRID_W), 1) < GRID_W

    slab, row_mask = [], []
    for j in range(NA_GROUP):
        r = g * NA_GROUP + j
        r_start = jnp.clip(r - kr // 2, 0, rows - kr)
        slab_j, mask_j = [], []
        for p in range(win // 2):
            a = w_start + 2 * p
            ok_l = (a >= r_start) & (a < r_start + kr)
            ok_r = (a + 1 >= r_start) & (a + 1 < r_start + kr)
            slab_j.append(jnp.clip(a - r + NA_ROWS_MAX, 0, 2 * NA_ROWS_MAX - 1))
            mask_j.append(jnp.where(left, jnp.where(ok_l, 0.0, MASK_VALUE), jnp.where(ok_r, 0.0, MASK_VALUE)))
        slab.append(slab_j)
        row_mask.append(jnp.concatenate(mask_j, axis=1))

    outs = []
    for h in range(N_HEADS):
        cols = slice(h * HEAD_DIM, (h + 1) * HEAD_DIM)
        qh = (q_ref[:, cols].astype(F32) * (scale * LOG2E)).astype(BF16)
        bias = jnp.concatenate(
            [jnp.concatenate([cb_ref[h, slab[j][p]] for p in range(win // 2)], axis=1) + row_mask[j]
             for j in range(NA_GROUP)], axis=0)
        s2 = _qk(qh, k_ref[pl.ds(base, nk), cols]) + bias
        outs.append(_softmax_pv(s2, v_ref[pl.ds(base, nk), cols]))
    o_ref[...] = jnp.concatenate(outs, axis=1).astype(o_ref.dtype)


def na_bias_slabs(rpb):
    cols = jnp.arange(GRID_W)
    col_start = jnp.clip(cols - NA_COLS // 2, 0, GRID_W - NA_COLS)
    col_in = (cols[None, :] >= col_start[:, None]) & (cols[None, :] < col_start[:, None] + NA_COLS)
    col_idx = jnp.clip(cols[None, :] - cols[:, None], -(NA_COLS - 1), NA_COLS - 1) + NA_COLS - 1
    col_bias = jnp.where(col_in[None, None], rpb[:, :, col_idx].astype(F32) * LOG2E, MASK_VALUE)
    pad = jnp.full((N_HEADS, 1, GRID_W, GRID_W), MASK_VALUE, F32)
    ext = jnp.concatenate([pad, col_bias, pad], axis=1)
    return jnp.concatenate([ext[:, :-1], ext[:, 1:]], axis=-1)


def na_attention(proj, slabs, batch, length):
    t = batch * length
    rows = length // GRID_W
    kr = min(NA_ROWS_MAX, rows)
    win = min(rows, kr + NA_GROUP)
    assert win % 2 == 0 and rows % NA_GROUP == 0
    steps = rows // NA_GROUP
    kv_spec = lambda off: pl.BlockSpec((length, BRANCH_WIDTH), lambda b, r: (b, off // BRANCH_WIDTH))
    return pl.pallas_call(
        functools.partial(_na_kernel, rows=rows, kr=kr, win=win, scale=HEAD_DIM ** -0.5),
        out_shape=jax.ShapeDtypeStruct((t, BRANCH_WIDTH), BF16),
        grid=(batch, steps),
        in_specs=[pl.BlockSpec((NA_GROUP * GRID_W, BRANCH_WIDTH),
                               lambda b, r: (b * steps + r, OFF_QNA // BRANCH_WIDTH)),
                  kv_spec(OFF_KNA), kv_spec(OFF_VNA),
                  pl.BlockSpec((N_HEADS, 2 * NA_ROWS_MAX, GRID_W, 2 * GRID_W), lambda b, r: (0, 0, 0, 0))],
        out_specs=pl.BlockSpec((NA_GROUP * GRID_W, BRANCH_WIDTH), lambda b, r: (b * steps + r, 0)),
        compiler_params=_params(("parallel", "parallel"), 48),
        name="na_attention",
    )(proj, proj, proj, slabs)


HY_TC = 1024
CONV_TC = 512


def _short_conv_kernel(u_ref, w_ref, b_ref, o_ref):
    u = u_ref[...].astype(F32)
    n = u.shape[0]
    row = lax.broadcasted_iota(jnp.int32, u.shape, 0)
    prev = jnp.where(row == 0, 0.0, pltpu.roll(u, 1, 0))
    nxt = jnp.where(row == n - 1, 0.0, pltpu.roll(u, n - 1, 0))
    w = w_ref[...]
    o_ref[...] = (prev * w[0:1] + u * w[1:2] + nxt * w[2:3] + b_ref[...]).astype(o_ref.dtype)


def short_conv(proj, w, bias, batch, length):
    t = batch * length
    width = 3 * HYENA_WIDTH
    return pl.pallas_call(
        _short_conv_kernel,
        out_shape=jax.ShapeDtypeStruct((t, width), BF16),
        grid=(batch, width // CONV_TC),
        in_specs=[pl.BlockSpec((length, CONV_TC), lambda b, c: (b, c)),
                  pl.BlockSpec((3, CONV_TC), lambda b, c: (0, c)),
                  pl.BlockSpec((1, CONV_TC), lambda b, c: (0, c))],
        out_specs=pl.BlockSpec((length, CONV_TC), lambda b, c: (b, c)),
        compiler_params=_params(("parallel", "parallel"), 48),
        name="short_conv",
    )(proj, w, bias.reshape(1, width))


def dft_matrices(length):
    n = 2 * length
    kb = min(512, length)
    nkb = length // kb
    k = jnp.arange(length, dtype=jnp.int32)[:, None]
    s = jnp.arange(length, dtype=jnp.int32)[None, :]
    ang = ((k * s) % n).astype(F32) * (2.0 * math.pi / n)
    c, sn = jnp.cos(ang), jnp.sin(ang)
    alt_s = jnp.where(s % 2 == 0, 1.0, -1.0).astype(F32)
    alt_t = jnp.where(k % 2 == 0, 1.0, -1.0).astype(F32)
    f_re = c.astype(BF16)
    f_im = jnp.where(k == 0, alt_s, -sn).astype(BF16)
    g_c = jnp.where(s == 0, 1.0 / n, c * (2.0 / n)).astype(BF16)
    g_s = jnp.where(s == 0, alt_t / n, -sn * (2.0 / n)).astype(BF16)
    blk = lambda i: slice(i * kb, (i + 1) * kb)
    f_fwd = jnp.concatenate([m[blk(i)] for i in range(nkb) for m in (f_re, f_im)], axis=0)
    g_inv = jnp.concatenate([m[:, blk(i)] for i in range(nkb) for m in (g_c, g_s)], axis=1)
    return f_fwd, g_inv


def hyena_filter_spectrum(length, w1, b1, w2, b2, w3, freq, decay, f_fwd):
    hp = lax.Precision.HIGHEST
    pos = jnp.arange(length, dtype=F32)
    tt = (pos / max(length - 1, 1))[:, None]
    bands = jnp.linspace(1e-4, HYENA_BANDS - 1, HYENA_BANDS, dtype=F32)
    ang = (2.0 * math.pi / length) * pos[:, None] * bands[None, :]
    feats = jnp.concatenate([tt, jnp.cos(ang), jnp.sin(ang)], axis=-1)
    h = jnp.sin(freq[0] * (jnp.dot(feats, w1, precision=hp) + b1))
    h = jnp.sin(freq[1] * (jnp.dot(h, w2, precision=hp) + b2))
    h = jnp.dot(h, w3, precision=hp).reshape(length, 2, 2, HYENA_WIDTH)
    h = h * jnp.exp(-tt[:, :, None, None] * jnp.abs(decay))
    width = 2 * HYENA_WIDTH
    fwd = h[:, :, 0].reshape(length, width)
    bwd = jnp.where(pos[:, None] == 0, 0.0, h[:, :, 1].reshape(length, width))
    inv_l1 = 1.0 / (jnp.sum(jnp.abs(fwd), axis=0) + jnp.sum(jnp.abs(bwd), axis=0))
    n = 2 * length
    kb = min(512, length)
    spec = matmul(f_fwd, jnp.concatenate([fwd, bwd], axis=1).astype(BF16), 2 * width, F32,
                  tm=min(1024, n), tn=512, name="filter_dft")
    re_rows = lambda i: slice(2 * i * kb, (2 * i + 1) * kb)
    im_rows = lambda i: slice((2 * i + 1) * kb, (2 * i + 2) * kb)
    nkb = length // kb
    k_re = jnp.concatenate([spec[re_rows(i), :width] + spec[re_rows(i), width:] for i in range(nkb)], axis=0) * inv_l1
    k_im = jnp.concatenate([spec[im_rows(i), :width] - spec[im_rows(i), width:] for i in range(nkb)], axis=0) * inv_l1
    nyquist = (spec[kb, :width] + spec[kb, width:]) * inv_l1
    first = jnp.arange(length)[:, None] == 0
    return k_re, jnp.where(first, 0.0, k_im), jnp.where(first, nyquist[None, :], k_re)


def _dft_fwd_kernel(f_ref, z_ref, a_ref, b_ref, a2_ref, o_ref):
    res = jnp.dot(f_ref[...], z_ref[...], preferred_element_type=F32)
    kb = res.shape[0] // 2
    z_re, z_im = res[:kb], res[kb:]
    b = b_ref[...]
    o_ref[:kb, :] = (z_re * a_ref[...] - z_im * b).astype(o_ref.dtype)
    o_ref[kb:, :] = (z_re * b + z_im * a2_ref[...]).astype(o_ref.dtype)


def dft_forward(f_fwd, z, z_col, coefs, order, batch, length):
    n = 2 * length
    kb = min(512, length)
    nkb = length // kb
    nc = HYENA_WIDTH // HY_TC
    coef = pl.BlockSpec((kb, HY_TC), lambda b, c, k: (k, order * nc + c))
    return pl.pallas_call(
        _dft_fwd_kernel,
        out_shape=jax.ShapeDtypeStruct((batch * n, HYENA_WIDTH), BF16),
        grid=(batch, nc, nkb),
        in_specs=[pl.BlockSpec((2 * kb, length), lambda b, c, k: (k, 0)),
                  pl.BlockSpec((length, HY_TC), lambda b, c, k: (b, z_col + c)),
                  coef, coef, coef],
        out_specs=pl.BlockSpec((2 * kb, HY_TC), lambda b, c, k: (b * nkb + k, c)),
        compiler_params=_params(("parallel", "parallel", "parallel"), 56),
        name="dft_forward",
    )(f_fwd, z, *coefs)


def _dft_inv_kernel(g_ref, y_ref, x_ref, z_ref, bias_ref, o_ref):
    y = jnp.dot(g_ref[...], y_ref[...], preferred_element_type=F32)
    o_ref[...] = (x_ref[...].astype(F32) * (y + bias_ref[...] * z_ref[...].astype(F32))).astype(o_ref.dtype)


def dft_inverse(g_inv, spec, gate, gate_col, z, z_col, bias, batch, length):
    n = 2 * length
    tt = min(1024, length)
    nt = length // tt
    nc = HYENA_WIDTH // HY_TC
    return pl.pallas_call(
        _dft_inv_kernel,
        out_shape=jax.ShapeDtypeStruct((batch * length, HYENA_WIDTH), BF16),
        grid=(batch, nc, nt),
        in_specs=[pl.BlockSpec((tt, n), lambda b, c, i: (i, 0)),
                  pl.BlockSpec((n, HY_TC), lambda b, c, i: (b, c)),
                  pl.BlockSpec((tt, HY_TC), lambda b, c, i: (b * nt + i, gate_col + c)),
                  pl.BlockSpec((tt, HY_TC), lambda b, c, i: (b * nt + i, z_col + c)),
                  pl.BlockSpec((1, HY_TC), lambda b, c, i: (0, c))],
        out_specs=pl.BlockSpec((tt, HY_TC), lambda b, c, i: (b * nt + i, c)),
        compiler_params=_params(("parallel", "parallel", "parallel"), 56),
        name="dft_inverse",
    )(g_inv, spec, gate, z, bias.reshape(1, HYENA_WIDTH))


def hyena_mixer(proj, conv_w, conv_b, coefs, bias, f_fwd, g_inv, batch, length):
    nc = HYENA_WIDTH // HY_TC
    cv = short_conv(proj, conv_w, conv_b, batch, length)
    spec = dft_forward(f_fwd, cv, 0, coefs, 0, batch, length)
    z1 = dft_inverse(g_inv, spec, cv, nc, cv, 0, bias[0], batch, length)
    spec = dft_forward(f_fwd, z1, 0, coefs, 1, batch, length)
    return dft_inverse(g_inv, spec, cv, 2 * nc, z1, 0, bias[1], batch, length)


def _merge_kernel(h_ref, g0, g1, g2, g3, y0, y1, y2, y3, wb_ref, o_ref):
    h = h_ref[...]
    acc = None
    for i, (g_ref, y_ref) in enumerate(((g0, y0), (g1, y1), (g2, y2), (g3, y3))):
        gate = jax.nn.sigmoid(jnp.dot(h, g_ref[...], preferred_element_type=F32))
        term = gate * jnp.dot(y_ref[...], wb_ref[i], preferred_element_type=F32)
        acc = term if acc is None else acc + term
    o_ref[...] = acc.astype(o_ref.dtype)


def gated_merge(h, w_gate, ys, w_branch, tm=512, tn=256):
    t, d = h.shape
    gate_spec = lambda i: pl.BlockSpec((d, tn), lambda n, m: (0, (i * d) // tn + n))
    y_spec = pl.BlockSpec((tm, BRANCH_WIDTH), lambda n, m: (m, 0))
    return pl.pallas_call(
        _merge_kernel,
        out_shape=jax.ShapeDtypeStruct((t, d), BF16),
        grid=(d // tn, t // tm),
        in_specs=[pl.BlockSpec((tm, d), lambda n, m: (m, 0))] + [gate_spec(i) for i in range(4)] + [y_spec] * 4
                 + [pl.BlockSpec((4, BRANCH_WIDTH, tn), lambda n, m: (0, 0, n))],
        out_specs=pl.BlockSpec((tm, tn), lambda n, m: (m, n)),
        compiler_params=_params(("parallel", "parallel"), 52),
        name="gated_merge",
    )(h, w_gate, w_gate, w_gate, w_gate, *ys, w_branch)


def moe_routing(logits, rg_b, re_b, tm, n_tiles):
    t = logits.shape[0]
    g_prob = jax.nn.softmax(logits[:, :N_GROUPS] + rg_b, axis=-1)
    g_idx = jnp.argmax(g_prob, axis=-1)
    g_val = jnp.max(g_prob, axis=-1)
    e_logit = (logits[:, N_GROUPS:N_GROUPS + N_EXPERTS] + re_b).reshape(t, N_GROUPS, EXPERTS_PER_GROUP)
    e_sel = jnp.take_along_axis(e_logit, g_idx[:, None, None], axis=1)[:, 0]
    top_val, top_idx = lax.top_k(jax.nn.softmax(e_sel, axis=-1), 2)
    top_val = top_val / jnp.sum(top_val, axis=-1, keepdims=True)
    weight = (g_val[:, None] * top_val).T.reshape(-1)
    expert = (g_idx[:, None] * EXPERTS_PER_GROUP + top_idx).T.reshape(-1).astype(jnp.int32)

    order = jnp.argsort(expert, stable=True).astype(jnp.int32)
    counts = jnp.sum(expert[None, :] == jnp.arange(N_EXPERTS, dtype=jnp.int32)[:, None], axis=1, dtype=jnp.int32)
    padded = ((counts + tm - 1) // tm) * tm
    pad_end = jnp.cumsum(padded)
    pad_start = pad_end - padded
    start = jnp.cumsum(counts) - counts
    tile_start = jnp.arange(n_tiles, dtype=jnp.int32) * tm
    tile_expert = jnp.minimum(jnp.sum(pad_end[None, :] <= tile_start[:, None], axis=1, dtype=jnp.int32), N_EXPERTS - 1)
    n_used = (pad_end[-1] // tm).astype(jnp.int32).reshape(1)
    in_tile = jnp.arange(tm, dtype=jnp.int32)[None, :]
    off = (tile_start - pad_start[tile_expert])[:, None] + in_tile
    valid = (off < counts[tile_expert][:, None]) & (tile_start < pad_end[-1])[:, None]
    src = order[jnp.clip(start[tile_expert][:, None] + off, 0, 2 * t - 1)]
    row_tok = jnp.where(valid, src % t, 0).astype(jnp.int32)
    row_dst = jnp.where(valid, src, 2 * t + in_tile).astype(jnp.int32)
    row_w = jnp.where(valid, weight[src], 0.0).astype(F32)
    return tile_expert, n_used, row_tok, row_dst, row_w


MOE_UNROLL = 32


def _moe_kernel(te_ref, nu_ref, tok_ref, tok_next_ref, dst_ref, w_ref, x_hbm, wg_ref, wu_ref, wd_ref, o_hbm,
                xbuf, obuf, gsem, ssem):
    tm = xbuf.shape[1]
    i = pl.program_id(0)
    n_used = nu_ref[0]
    slot = i % 2

    def issue_gather(idx_ref, s):
        @pl.loop(0, tm // MOE_UNROLL)
        def _(c):
            for j in range(MOE_UNROLL):
                r = c * MOE_UNROLL + j
                pltpu.make_async_copy(x_hbm.at[pl.ds(idx_ref[0, 0, r], 1)], xbuf.at[s, pl.ds(r, 1)],
                                      gsem.at[s]).start(priority=j % 2)

    def wait_gather(s):
        pltpu.make_async_copy(x_hbm.at[pl.ds(0, tm)], xbuf.at[s], gsem.at[s]).wait()

    def wait_scatter(s):
        pltpu.make_async_copy(obuf.at[s], o_hbm.at[pl.ds(0, tm)], ssem.at[s]).wait()

    @pl.when(i < n_used)
    def _():
        @pl.when(i == 0)
        def _():
            issue_gather(tok_ref, slot)
            obuf[1] = jnp.zeros(obuf.shape[1:], obuf.dtype)
            init = pltpu.make_async_copy(obuf.at[1], o_hbm.at[pl.ds(o_hbm.shape[0] - tm, tm)], ssem.at[1])
            init.start()
            init.wait()

        wait_gather(slot)
        xn = unpack_halves(xbuf[slot]).astype(BF16)
        issue_gather(tok_next_ref, 1 - slot)
        a = jnp.dot(xn, wg_ref[0], preferred_element_type=F32)
        u = jnp.dot(xn, wu_ref[0], preferred_element_type=F32)
        hid = (a * jax.nn.sigmoid(a) * u * w_ref[...]).astype(BF16)
        res = pack_halves(jnp.dot(hid, wd_ref[0], preferred_element_type=F32))

        @pl.when(i >= 2)
        def _():
            wait_scatter(slot)

        obuf[slot] = res

        @pl.loop(0, tm // MOE_UNROLL)
        def _(c):
            for j in range(MOE_UNROLL):
                r = c * MOE_UNROLL + j
                pltpu.make_async_copy(obuf.at[slot, pl.ds(r, 1)], o_hbm.at[pl.ds(dst_ref[0, 0, r], 1)],
                                      ssem.at[slot]).start(priority=j % 2)

        @pl.when(i == n_used - 1)
        def _():
            wait_gather(1 - slot)
            wait_scatter(slot)

            @pl.when(i >= 1)
            def _():
                wait_scatter(1 - slot)


def moe_experts(xp, schedule, w_gate, w_up, w_down, layer, tm):
    t, dp = xp.shape
    d = 2 * dp
    tile_expert, n_used, row_tok, row_dst, row_w = schedule
    n_tiles = tile_expert.shape[0]
    w_idx = lambda i, te, nu: (layer * N_EXPERTS + te[i], 0, 0)
    idx_spec = lambda off: pl.BlockSpec((1, 1, tm), lambda i, te, nu: (jnp.minimum(i + off, n_tiles - 1), 0, 0),
                                        memory_space=pltpu.SMEM)
    grid_spec = pltpu.PrefetchScalarGridSpec(
        num_scalar_prefetch=2,
        grid=(n_tiles,),
        in_specs=[idx_spec(0), idx_spec(1), idx_spec(0),
                  pl.BlockSpec((tm, 1), lambda i, te, nu: (i, 0)),
                  pl.BlockSpec(memory_space=pl.ANY),
                  pl.BlockSpec((1, d, EXPERT_FF), w_idx),
                  pl.BlockSpec((1, d, EXPERT_FF), w_idx),
                  pl.BlockSpec((1, EXPERT_FF, d), w_idx)],
        out_specs=pl.BlockSpec(memory_space=pl.ANY),
        scratch_shapes=[pltpu.VMEM((2, tm, dp), jnp.uint32), pltpu.VMEM((2, tm, dp), jnp.uint32),
                        pltpu.SemaphoreType.DMA((2,)), pltpu.SemaphoreType.DMA((2,))],
    )
    tok3 = row_tok.reshape(n_tiles, 1, tm)
    return pl.pallas_call(
        _moe_kernel,
        out_shape=jax.ShapeDtypeStruct((2 * t + tm, dp), jnp.uint32),
        grid_spec=grid_spec,
        compiler_params=_params(("arbitrary",), 48),
        name="moe_experts",
    )(tile_expert, n_used, tok3, tok3, row_dst.reshape(n_tiles, 1, tm),
      row_w.reshape(n_tiles * tm, 1), xp, w_gate, w_up, w_down)


def split_w_in(w):
    d = w.shape[0]
    cols = lambda a, b: w[:, a:b].astype(BF16)
    z = lambda n: jnp.zeros((d, n), BF16)
    w_mla = jnp.concatenate([cols(IN_CQ, IN_CKV), cols(IN_KR, IN_HY), z(64), z(128), cols(IN_CKV, IN_KR)], axis=1)
    return cols(IN_QKV, IN_GATE), cols(IN_HY, IN_QKV), w_mla, cols(IN_GATE, IN_END)


def pack_w_uq(w):
    r = w.shape[0]
    w = w.reshape(r, N_HEADS, HEAD_DIM + MLA_ROPE_DIM)
    w = jnp.concatenate([w, jnp.zeros((r, N_HEADS, HEAD_DIM - MLA_ROPE_DIM), w.dtype)], axis=-1)
    return w.reshape(r, N_HEADS * 2 * HEAD_DIM).astype(BF16)


def pack_router(rg_w, re_w):
    d = rg_w.shape[0]
    pad = jnp.zeros((d, HEAD_DIM - N_GROUPS - N_EXPERTS), rg_w.dtype)
    return jnp.concatenate([rg_w, re_w, pad], axis=1).astype(BF16)


def kernel(x, norm_mix_g, w_in, mla_q_norm_g, mla_kv_norm_g, mla_w_uq, mla_w_ukv, hyena_conv_w, hyena_conv_b, hyena_ffn_w1, hyena_ffn_b1, hyena_ffn_w2, hyena_ffn_b2, hyena_ffn_w3, hyena_sin_freq, hyena_decay, hyena_bias, diff_lambda, diff_subln_g, na_rpb, w_branch, w_out, norm_ffn_g, router_group_w, router_group_b, router_expert_w, router_expert_b, moe_w_gate, moe_w_up, moe_w_down, norm_final_g):
    batch, length, d = x.shape
    t = batch * length
    assert d == D_MODEL and length % GRID_W == 0 and t % 1024 == 0
    tq_mla, tq_diff = min(1024, length), min(1024, length)
    n_tiles = 2 * t // MOE_TM + N_EXPERTS

    rope_mla = rope_tables(length, 1)
    rope_diff = rope_tables(length, 2)
    f_fwd, g_inv = dft_matrices(length)
    moe_wg = moe_w_gate.astype(BF16).reshape(DEPTH * N_EXPERTS, d, EXPERT_FF)
    moe_wu = moe_w_up.astype(BF16).reshape(DEPTH * N_EXPERTS, d, EXPERT_FF)
    moe_wd = moe_w_down.astype(BF16).reshape(DEPTH * N_EXPERTS, EXPERT_FF, d)

    x2 = x.reshape(t, d)
    h = rms_norm(x2, norm_mix_g[0], BF16)
    out = None
    for l in range(DEPTH):
        w_qkv, w_hy, w_mla, w_gate = split_w_in(w_in[l])
        proj_qkv = matmul(h, w_qkv, N_QKV, BF16, tm=1024, tn=512, name="in_proj_qkv")
        proj_hy = matmul(h, w_hy, N_HY, BF16, tm=1024, tn=512, name="in_proj_hy")
        proj_mla = matmul(h, w_mla, N_MLA, BF16, tm=1024, tn=512, name="in_proj_mla")

        qf = norm_matmul(proj_mla, OFF_CQ // MLA_Q_RANK, MLA_Q_RANK, mla_q_norm_g[l], pack_w_uq(mla_w_uq[l]), BF16,
                         tm=512, name="mla_q_up")
        kv = norm_matmul(proj_mla, OFF_CKV // MLA_KV_RANK, MLA_KV_RANK, mla_kv_norm_g[l], mla_w_ukv[l].astype(BF16),
                         BF16, tm=512, name="mla_kv_up")
        y_a = mla_attention(qf, kv, proj_mla, rope_mla, batch, length, tq_mla)

        coefs = hyena_filter_spectrum(length, hyena_ffn_w1[l], hyena_ffn_b1[l], hyena_ffn_w2[l], hyena_ffn_b2[l],
                                      hyena_ffn_w3[l], hyena_sin_freq[l], hyena_decay[l], f_fwd)
        y_b = hyena_mixer(proj_hy, hyena_conv_w[l], hyena_conv_b[l], coefs, hyena_bias[l], f_fwd, g_inv, batch, length)

        lambda_init = 0.8 - 0.6 * math.exp(-0.3 * l)
        lam = diff_lambda[l].astype(F32)
        lam_full = jnp.exp(jnp.sum(lam[0] * lam[1])) - jnp.exp(jnp.sum(lam[2] * lam[3])) + lambda_init
        y_c = diff_attention(proj_qkv, rope_diff, lam_full, diff_subln_g[l], lambda_init, batch, length, tq_diff)

        y_d = na_attention(proj_qkv, na_bias_slabs(na_rpb[l]), batch, length)

        merged = gated_merge(h, w_gate, (y_a, y_b, y_c, y_d), w_branch[l].astype(BF16))
        x_mid = matmul(merged, w_out[l].astype(BF16), d, F32, tm=1024, tn=512, res=x2, name="out_proj")

        logits, xp = router(x_mid, norm_ffn_g[l], pack_router(router_group_w[l], router_expert_w[l]))
        schedule = moe_routing(logits, router_group_b[l], router_expert_b[l], MOE_TM, n_tiles)
        out2 = moe_experts(xp, schedule, moe_wg, moe_wu, moe_wd, l, MOE_TM)
        if l + 1 < DEPTH:
            x2, h = combine_rms_norm(x_mid, out2, norm_mix_g[l + 1], BF16, emit_x=True)
        else:
            out = combine_rms_norm(x_mid, out2, norm_final_g, F32, emit_x=False)[0]
    return out.reshape(batch, length, d)
```
